```python
import math
import jax, jax.numpy as jnp
from jax import lax
import numpy as np

D_MODEL = 1024
BATCH = 2
SEQ = 16384
DEPTH = 4
DEC_BATCH = 32
DEC_SEQ = 32
PAST_LEN = 2048

CHUNK = 64
Q_BLOCK = 128
ATTN_WIDTH = D_MODEL // 2
POOL_WIDTH = D_MODEL // 4
CONV_WIDTH = D_MODEL // 4
MIX_WIDTH = ATTN_WIDTH + POOL_WIDTH + CONV_WIDTH
HEAD_DIM = 64
V_DIM = 2 * HEAD_DIM
N_HEADS = ATTN_WIDTH // V_DIM
N_SUB = 2 * N_HEADS
POOL_WINDOWS = (2, 4, 8, 16)
N_POOL_GROUPS = len(POOL_WINDOWS)
POOL_GROUP = POOL_WIDTH // N_POOL_GROUPS
POOL_STATE = max(POOL_WINDOWS) - 1
CONV_K = 31
CONV_STATE = CONV_K - 1
IN_WIDTH = 3 * ATTN_WIDTH + POOL_WIDTH + 2 * CONV_WIDTH
D_FF = 2816
N_EXPERTS = 8
TOP_K = 2
D_FF_EXPERT = 3584
N_DENSE = (DEPTH + 1) // 2
N_MOE = DEPTH // 2
EPS = 1e-6

kernel_name = 'hybrid_diffattn_pool_conformer_stream_step'


def rms_norm(x, g):
    x32 = x.astype(jnp.float32)
    y = x32 * lax.rsqrt(jnp.mean(x32 * x32, axis=-1, keepdims=True) + EPS)
    return (y * g.astype(jnp.float32)).astype(x.dtype)


def layer_norm(x, g, b):
    x32 = x.astype(jnp.float32)
    mu = jnp.mean(x32, axis=-1, keepdims=True)
    xc = x32 - mu
    y = xc * lax.rsqrt(jnp.mean(xc * xc, axis=-1, keepdims=True) + EPS)
    return (y * g.astype(jnp.float32) + b.astype(jnp.float32)).astype(x.dtype)


def diff_attn_prompt(q, k, v, lam):
    b, s = q.shape[0], q.shape[1]
    nb = s // Q_BLOCK
    scale = HEAD_DIM ** -0.5
    kpos = jnp.arange(s)
    qb = jnp.moveaxis(q.reshape(b, nb, Q_BLOCK, N_HEADS, 2, HEAD_DIM), 1, 0)

    def block(args):
        q_blk, i = args
        qpos = i * Q_BLOCK + jnp.arange(Q_BLOCK)
        limit = (qpos // CHUNK + 1) * CHUNK
        mask = kpos[None, :] < limit[:, None]
        sc = jnp.einsum('bqhcd,bkhcd->bhcqk', q_blk, k,
                        preferred_element_type=jnp.float32) * scale
        sc = jnp.where(mask, sc, -jnp.inf)
        p = jax.nn.softmax(sc, axis=-1)
        a = p[:, :, 0] - lam * p[:, :, 1]
        return jnp.einsum('bhqk,bkhe->bqhe', a.astype(v.dtype), v)

    o = lax.map(block, (qb, jnp.arange(nb)))
    return jnp.moveaxis(o, 0, 1).reshape(b, s, N_HEADS, V_DIM)


def diff_attn_sample(q, k, v, k_cache, v_cache, lam):
    db, p_len = k_cache.shape[0], k_cache.shape[1]
    scale = HEAD_DIM ** -0.5
    kc = k_cache.reshape(db, p_len, N_HEADS, 2, HEAD_DIM)
    s_past = jnp.einsum('bqhcd,bkhcd->bhcqk', q, kc, preferred_element_type=jnp.float32)
    s_new = jnp.einsum('bqhcd,bkhcd->bhcqk', q, k, preferred_element_type=jnp.float32)
    sc = jnp.concatenate([s_past, s_new], axis=-1) * scale
    p = jax.nn.softmax(sc, axis=-1)
    a = (p[:, :, 0] - lam * p[:, :, 1]).astype(v.dtype)
    return (jnp.einsum('bhqk,bkhe->bqhe', a[..., :p_len], v_cache)
            + jnp.einsum('bhqk,bkhe->bqhe', a[..., p_len:], v))


def pool_mix(u, prev, pos0, pool_w, pool_scale):
    b, L, c = u.shape
    padded = jnp.concatenate([prev.astype(u.dtype), u], axis=1)
    cs = jnp.concatenate([jnp.zeros((b, 1, c), jnp.float32),
                          jnp.cumsum(padded.astype(jnp.float32), axis=1)], axis=1)
    pos = pos0 + jnp.arange(L)
    end = cs[:, POOL_STATE + 1:]
    means = []
    for g, win in enumerate(POOL_WINDOWS):
        sl = slice(g * POOL_GROUP, (g + 1) * POOL_GROUP)
        start = cs[:, POOL_STATE + 1 - win: POOL_STATE + 1 - win + L, sl]
        cnt = jnp.minimum(pos + 1, win).astype(jnp.float32)[None, :, None]
        means.append((end[..., sl] - start) / cnt)
    d = (jnp.concatenate(means, axis=-1) - u.astype(jnp.float32))
    d = d.reshape(b, L, N_POOL_GROUPS, POOL_GROUP)
    y = jnp.einsum('blgc,gce->blge', d, pool_w).reshape(b, L, c) * pool_scale
    return y.astype(u.dtype), padded[:, -POOL_STATE:]


def conv_mix(c_in, prev, conv_w, conv_b, ln_g, ln_b, conv_pw):
    a, gate = jnp.split(c_in, 2, axis=-1)
    g = a * jax.nn.sigmoid(gate)
    padded = jnp.concatenate([prev.astype(g.dtype), g], axis=1)
    y = lax.conv_general_dilated(padded, conv_w[:, None, :].astype(g.dtype),
                                 window_strides=(1,), padding='VALID',
                                 dimension_numbers=('NWC', 'WIO', 'NWC'),
                                 feature_group_count=CONV_WIDTH) + conv_b
    y = jax.nn.silu(layer_norm(y, ln_g, ln_b))
    return y @ conv_pw, padded[:, -CONV_STATE:]


def mixer_block(h, kv_cache, pool_prev, conv_prev, pos0, lam, lam_init,
                w_in, q_norm_g, k_norm_g, subln_g, pool_w, pool_scale,
                conv_w, conv_b, conv_ln_g, conv_ln_b, conv_pw, w_out):
    b, L, _ = h.shape
    z = h @ w_in
    o1, o2, o3 = ATTN_WIDTH, 2 * ATTN_WIDTH, 3 * ATTN_WIDTH
    o4 = o3 + POOL_WIDTH
    q = rms_norm(z[..., :o1].reshape(b, L, N_HEADS, 2, HEAD_DIM), q_norm_g)
    k = rms_norm(z[..., o1:o2].reshape(b, L, N_HEADS, 2, HEAD_DIM), k_norm_g)
    v = z[..., o2:o3].reshape(b, L, N_HEADS, V_DIM)
    u = z[..., o3:o4]
    c = z[..., o4:]
    if kv_cache is None:
        att = diff_attn_prompt(q, k, v, lam)
    else:
        att = diff_attn_sample(q, k, v, kv_cache[0], kv_cache[1], lam)
    att = (rms_norm(att, subln_g) * (1.0 - lam_init)).reshape(b, L, ATTN_WIDTH)
    pool_out, pool_state = pool_mix(u, pool_prev, pos0, pool_w, pool_scale)
    conv_out, conv_state = conv_mix(c, conv_prev, conv_w, conv_b, conv_ln_g, conv_ln_b, conv_pw)
    mixed = jnp.concatenate([att.astype(h.dtype), pool_out.astype(h.dtype),
                             conv_out.astype(h.dtype)], axis=-1)
    return (mixed @ w_out, k.reshape(b, L, N_SUB, HEAD_DIM), v, pool_state, conv_state)


def swiglu(h, wg, wu, wd):
    return (jax.nn.silu(h @ wg) * (h @ wu)) @ wd


def moe_swiglu(h, router_w, wg, wu, wd):
    logits = jnp.einsum('bld,de->ble', h, router_w, preferred_element_type=jnp.float32)
    top_v, top_i = lax.top_k(logits, TOP_K)
    wts = jax.nn.softmax(top_v, axis=-1)
    gates = jnp.sum(jax.nn.one_hot(top_i, N_EXPERTS, dtype=jnp.float32) * wts[..., None], axis=-2)
    gates = gates.astype(h.dtype)
    out = jnp.zeros_like(h)
    for e in range(N_EXPERTS):
        out = out + gates[..., e:e + 1] * swiglu(h, wg[e], wu[e], wd[e])
    return out


def setup_inputs(seed: int = 0) -> dict:
    key = jax.random.key(seed)
    ks = jax.random.split(key, 40)
    f32 = jnp.float32

    def nrm(k, shape, scale):
        return jax.random.normal(k, shape, f32) * scale

    def gain(k, shape):
        return 1.0 + 0.02 * jax.random.normal(k, shape, f32)

    return {
        'x_prompt': nrm(ks[0], (BATCH, SEQ, D_MODEL), 1.0),
        'x_sample': nrm(ks[1], (DEC_BATCH, DEC_SEQ, D_MODEL), 1.0),
        'cache_k': nrm(ks[2], (DEPTH, DEC_BATCH, PAST_LEN, N_SUB, HEAD_DIM), 1.0),
        'cache_v': nrm(ks[3], (DEPTH, DEC_BATCH, PAST_LEN, N_HEADS, V_DIM), 1.0),
        'state_pool': nrm(ks[4], (DEPTH, DEC_BATCH, POOL_STATE, POOL_WIDTH), 1.0),
        'state_conv': nrm(ks[5], (DEPTH, DEC_BATCH, CONV_STATE, CONV_WIDTH), 0.5),
        'attn_norm_g': gain(ks[6], (DEPTH, D_MODEL)),
        'w_in': nrm(ks[7], (DEPTH, D_MODEL, IN_WIDTH), D_MODEL ** -0.5),
        'q_norm_g': gain(ks[8], (DEPTH, HEAD_DIM)),
        'k_norm_g': gain(ks[9], (DEPTH, HEAD_DIM)),
        'lambda_q1': nrm(ks[10], (DEPTH, HEAD_DIM), 0.1),
        'lambda_k1': nrm(ks[11], (DEPTH, HEAD_DIM), 0.1),
        'lambda_q2': nrm(ks[12], (DEPTH, HEAD_DIM), 0.1),
        'lambda_k2': nrm(ks[13], (DEPTH, HEAD_DIM), 0.1),
        'subln_g': gain(ks[14], (DEPTH, V_DIM)),
        'pool_w': nrm(ks[15], (DEPTH, N_POOL_GROUPS, POOL_GROUP, POOL_GROUP), POOL_GROUP ** -0.5),
        'pool_scale': gain(ks[16], (DEPTH, POOL_WIDTH)),
        'conv_w': nrm(ks[17], (DEPTH, CONV_K, CONV_WIDTH), CONV_K ** -0.5),
        'conv_b': nrm(ks[18], (DEPTH, CONV_WIDTH), 0.02),
        'conv_ln_g': gain(ks[19], (DEPTH, CONV_WIDTH)),
        'conv_ln_b': nrm(ks[20], (DEPTH, CONV_WIDTH), 0.02),
        'conv_pw': nrm(ks[21], (DEPTH, CONV_WIDTH, CONV_WIDTH), CONV_WIDTH ** -0.5),
        'w_out': nrm(ks[22], (DEPTH, MIX_WIDTH, D_MODEL), MIX_WIDTH ** -0.5),
        'ffn_norm_g': gain(ks[23], (DEPTH, D_MODEL)),
        'ffn_w_gate': nrm(ks[24], (N_DENSE, D_MODEL, D_FF), D_MODEL ** -0.5),
        'ffn_w_up': nrm(ks[25], (N_DENSE, D_MODEL, D_FF), D_MODEL ** -0.5),
        'ffn_w_down': nrm(ks[26], (N_DENSE, D_FF, D_MODEL), D_FF ** -0.5),
        'router_w': nrm(ks[27], (N_MOE, D_MODEL, N_EXPERTS), D_MODEL ** -0.5),
        'moe_w_gate': nrm(ks[28], (N_MOE, N_EXPERTS, D_MODEL, D_FF_EXPERT), D_MODEL ** -0.5),
        'moe_w_up': nrm(ks[29], (N_MOE, N_EXPERTS, D_MODEL, D_FF_EXPERT), D_MODEL ** -0.5),
        'moe_w_down': nrm(ks[30], (N_MOE, N_EXPERTS, D_FF_EXPERT, D_MODEL), D_FF_EXPERT ** -0.5),
    }


def reference(x_prompt, x_sample, cache_k, cache_v, state_pool, state_conv,
              attn_norm_g, w_in, q_norm_g, k_norm_g, lambda_q1, lambda_k1, lambda_q2, lambda_k2,
              subln_g, pool_w, pool_scale, conv_w, conv_b, conv_ln_g, conv_ln_b, conv_pw, w_out,
              ffn_norm_g, ffn_w_gate, ffn_w_up, ffn_w_down,
              router_w, moe_w_gate, moe_w_up, moe_w_down):
    xp, xs = x_prompt, x_sample
    bp, bs = xp.shape[0], xs.shape[0]
    kp_l, vp_l, pp_l, cp_l = [], [], [], []
    ks_l, vs_l, ps_l, cs_l = [], [], [], []
    for l in range(DEPTH):
        lam_init = 0.8 - 0.6 * math.exp(-0.3 * l)
        lam = (jnp.exp(jnp.sum(lambda_q1[l].astype(jnp.float32) * lambda_k1[l].astype(jnp.float32)))
               - jnp.exp(jnp.sum(lambda_q2[l].astype(jnp.float32) * lambda_k2[l].astype(jnp.float32)))
               + lam_init)
        lw = (w_in[l], q_norm_g[l], k_norm_g[l], subln_g[l], pool_w[l], pool_scale[l],
              conv_w[l], conv_b[l], conv_ln_g[l], conv_ln_b[l], conv_pw[l], w_out[l])
        mp, kp, vp, pp, cp = mixer_block(
            rms_norm(xp, attn_norm_g[l]), None,
            jnp.zeros((bp, POOL_STATE, POOL_WIDTH), xp.dtype),
            jnp.zeros((bp, CONV_STATE, CONV_WIDTH), xp.dtype), 0, lam, lam_init, *lw)
        ms, kk, vs, ps, cs = mixer_block(
            rms_norm(xs, attn_norm_g[l]), (cache_k[l], cache_v[l]),
            state_pool[l], state_conv[l], PAST_LEN, lam, lam_init, *lw)
        xp = xp + mp
        xs = xs + ms
        kp_l.append(kp); vp_l.append(vp); pp_l.append(pp); cp_l.append(cp)
        ks_l.append(kk); vs_l.append(vs); ps_l.append(ps); cs_l.append(cs)
        i = l // 2
        hp = rms_norm(xp, ffn_norm_g[l])
        hs = rms_norm(xs, ffn_norm_g[l])
        if l % 2 == 0:
            xp = xp + swiglu(hp, ffn_w_gate[i], ffn_w_up[i], ffn_w_down[i])
            xs = xs + swiglu(hs, ffn_w_gate[i], ffn_w_up[i], ffn_w_down[i])
        else:
            xp = xp + moe_swiglu(hp, router_w[i], moe_w_gate[i], moe_w_up[i], moe_w_down[i])
            xs = xs + moe_swiglu(hs, router_w[i], moe_w_gate[i], moe_w_up[i], moe_w_down[i])
    return (xp, xs,
            jnp.stack(kp_l), jnp.stack(vp_l), jnp.stack(pp_l), jnp.stack(cp_l),
            jnp.stack(ks_l), jnp.stack(vs_l), jnp.stack(ps_l), jnp.stack(cs_l))
```

```python
import functools
import math

import jax
import jax.numpy as jnp
from jax import lax
from jax.experimental import pallas as pl
from jax.experimental.pallas import tpu as pltpu

F32 = jnp.float32
BF16 = jnp.bfloat16
I32 = jnp.int32

EPS = 1e-6
CHUNK = 64
HEAD_DIM = 64
V_DIM = 2 * HEAD_DIM
POOL_WINDOWS = (2, 4, 8, 16)
POOL_HALO = 16
CONV_K = 31
CONV_HALO = 32
TOP_K = 2
NEG_BIG = -1e30

LANES = 128
MIB = 1024 * 1024
VMEM_CAP_BYTES = 60 * MIB


def _pick_tile(n, candidates):
    for c in candidates:
        if n % c == 0:
            return c
    raise ValueError(f"no tile in {candidates} divides {n}")


def _params(n_grid_dims, vmem_bytes):
    return pltpu.CompilerParams(dimension_semantics=("arbitrary",) * n_grid_dims,
                                vmem_limit_bytes=int(min(vmem_bytes, VMEM_CAP_BYTES)))


def _nt_dot(a, b):
    return lax.dot_general(a, b, (((1,), (1,)), ((), ())), preferred_element_type=F32)


def _dot(a, b):
    return jnp.dot(a, b, preferred_element_type=F32)


def _split_bf16(x):
    hi = x.astype(BF16)
    lo = (x - hi.astype(F32)).astype(BF16)
    return hi, lo


def _rms_rows(x, gain):
    ms = jnp.mean(x * x, axis=-1, keepdims=True)
    return x * lax.rsqrt(ms + EPS) * gain


def _in_proj_kernel(x_ref, ng_ref, w_ref, gsum_ref, qg_ref, kg_ref,
                    qa_ref, qb_ref, k32_ref, kb_ref, v32_ref, vx_ref, u_ref, g_ref,
                    *, aw, pw, cw):
    h = _rms_rows(x_ref[...], ng_ref[...]).astype(BF16)

    def proj(lo, width):
        return _dot(h, w_ref[:, lo:lo + width])

    def head_norm(t, gain):
        hi, lo = _split_bf16(t * t)
        ss = _dot(hi, gsum_ref[...]) + _dot(lo, gsum_ref[...])
        return t * lax.rsqrt(ss * (1.0 / HEAD_DIM) + EPS) * gain

    q = head_norm(proj(0, aw), qg_ref[...])
    lane = lax.broadcasted_iota(I32, q.shape, 1)
    first_map = (lane // HEAD_DIM) % 2 == 0
    qa_ref[...] = jnp.where(first_map, q, 0.0).astype(BF16)
    qb_ref[...] = jnp.where(first_map, 0.0, q).astype(BF16)

    k = head_norm(proj(aw, aw), kg_ref[...])
    k32_ref[...] = k
    kb_ref[...] = k.astype(BF16)

    v = proj(2 * aw, aw)
    v32_ref[...] = v
    vb = v.astype(BF16)
    ones = jnp.ones((vb.shape[0], V_DIM), BF16)
    for hh in range(aw // V_DIM):
        vx_ref[:, 2 * hh * V_DIM:(2 * hh + 1) * V_DIM] = vb[:, hh * V_DIM:(hh + 1) * V_DIM]
        vx_ref[:, (2 * hh + 1) * V_DIM:(2 * hh + 2) * V_DIM] = ones

    u_ref[...] = proj(3 * aw, pw)
    a = proj(3 * aw + pw, cw)
    gate = proj(3 * aw + pw + cw, cw)
    g_ref[...] = a * jax.nn.sigmoid(gate)


def _in_proj(x, norm_g, w_in, gsum, qg, kg, *, aw, pw, cw):
    t, d = x.shape
    tm = _pick_tile(t, (512, 256, 128))
    in_w = w_in.shape[1]
    row = lambda w: pl.BlockSpec((tm, w), lambda i: (i, 0))
    full = lambda a: pl.BlockSpec(a.shape, lambda i: (0,) * a.ndim)
    out_widths = (aw, aw, aw, aw, aw, 2 * aw, pw, cw)
    out_dtypes = (BF16, BF16, F32, BF16, F32, BF16, F32, F32)
    vmem = 2 * (tm * d * 4 + d * in_w * 2 + sum(tm * w * jnp.dtype(dt).itemsize
                                                 for w, dt in zip(out_widths, out_dtypes)))
    vmem += 8 * tm * aw * 4 + 2 * aw * aw * 2
    return pl.pallas_call(
        functools.partial(_in_proj_kernel, aw=aw, pw=pw, cw=cw),
        grid=(t // tm,),
        in_specs=[row(d), full(norm_g), full(w_in), full(gsum), full(qg), full(kg)],
        out_specs=[row(w) for w in out_widths],
        out_shape=[jax.ShapeDtypeStruct((t, w), dt) for w, dt in zip(out_widths, out_dtypes)],
        compiler_params=_params(1, vmem + 8 * MIB),
        name="in_proj",
    )(x, norm_g, w_in, gsum, qg, kg)


def _diff_finish(o, lam, sub_gain, rows):
    a = o[:rows] - lam * o[rows:]
    return _rms_rows(a, sub_gain)


def _attn_prompt_kernel(lam_ref, qa_ref, qb_ref, k_ref, vx_ref, sg_ref, o_ref,
                        q_sc, m_sc, acc_sc, *, tq):
    i = pl.program_id(2)
    q_sc[0:tq] = qa_ref[...]
    q_sc[tq:2 * tq] = qb_ref[...]
    m_sc[...] = jnp.full(m_sc.shape, NEG_BIG, F32)
    acc_sc[...] = jnp.zeros(acc_sc.shape, F32)

    def step(j, masked):
        start = pl.multiple_of(j * tq, tq)
        kj = k_ref[pl.ds(start, tq), :]
        vj = vx_ref[pl.ds(start, tq), :]
        s = _nt_dot(q_sc[...], kj)
        if masked:
            r = lax.broadcasted_iota(I32, s.shape, 0) % tq
            c = lax.broadcasted_iota(I32, s.shape, 1)
            s = jnp.where(c // CHUNK <= r // CHUNK, s, NEG_BIG)
        m_old = m_sc[...]
        m_new = jnp.maximum(m_old, jnp.max(s, axis=-1, keepdims=True))
        alpha = jnp.exp(m_old - m_new)
        p = jnp.exp(s - jnp.tile(m_new, (1, tq // LANES)))
        acc_sc[...] = acc_sc[...] * jnp.tile(alpha, (1, 2 * V_DIM // LANES)) + _dot(p.astype(BF16), vj)
        m_sc[...] = m_new

    def full_step(j, carry):
        step(j, masked=False)
        return carry

    lax.fori_loop(0, i, full_step, 0)
    step(i, masked=True)

    acc = acc_sc[...]
    o = acc[:, :V_DIM] / acc[:, V_DIM:]
    o_ref[...] = _diff_finish(o, lam_ref[0], sg_ref[...], tq).astype(BF16)


def _attn_prompt(lam, qa, qb, kb, vx, sub_gain, *, n_heads, b, s):
    tq = _pick_tile(s, (512, 256, 128))
    nq = s // tq
    qspec = pl.BlockSpec((tq, V_DIM), lambda bi, hi, qi: (bi * nq + qi, hi))
    vmem = 2 * (2 * tq * V_DIM * 2 + s * V_DIM * 2 + s * 2 * V_DIM * 2 + tq * V_DIM * 2)
    vmem += 2 * tq * (V_DIM * 2 + LANES * 4 + 2 * V_DIM * 4) + 6 * 2 * tq * tq * 4
    return pl.pallas_call(
        functools.partial(_attn_prompt_kernel, tq=tq),
        grid=(b, n_heads, nq),
        in_specs=[pl.BlockSpec(memory_space=pltpu.SMEM), qspec, qspec,
                  pl.BlockSpec((s, V_DIM), lambda bi, hi, qi: (bi, hi)),
                  pl.BlockSpec((s, 2 * V_DIM), lambda bi, hi, qi: (bi, hi)),
                  pl.BlockSpec((1, V_DIM), lambda bi, hi, qi: (0, 0))],
        out_specs=qspec,
        out_shape=jax.ShapeDtypeStruct((b * s, n_heads * V_DIM), BF16),
        scratch_shapes=[pltpu.VMEM((2 * tq, V_DIM), BF16), pltpu.VMEM((2 * tq, LANES), F32),
                        pltpu.VMEM((2 * tq, 2 * V_DIM), F32)],
        compiler_params=_params(3, vmem + 8 * MIB),
        name="attn_prompt",
    )(lam, qa, qb, kb, vx, sub_gain)


def _attn_sample_kernel(lam_ref, qa_ref, qb_ref, kn_ref, vn_ref, ck_ref, cv_ref, sg_ref, o_ref,
                        *, n_heads, ds):
    for hh in range(n_heads):
        cols = slice(hh * V_DIM, (hh + 1) * V_DIM)
        q2 = jnp.concatenate([qa_ref[:, cols], qb_ref[:, cols]], axis=0)
        s_past = _nt_dot(q2, ck_ref[0, 0, :, cols].astype(BF16))
        s_new = _nt_dot(q2, kn_ref[:, cols])
        m = jnp.maximum(jnp.max(s_past, axis=-1, keepdims=True),
                        jnp.max(s_new, axis=-1, keepdims=True))
        p_past = jnp.exp(s_past - m)
        p_new = jnp.exp(s_new - m)
        denom = jnp.sum(p_past, axis=-1, keepdims=True) + jnp.sum(p_new, axis=-1, keepdims=True)
        pv = (_dot(p_past.astype(BF16), cv_ref[0, 0, :, cols].astype(BF16))
              + _dot(p_new.astype(BF16), vn_ref[:, 2 * hh * V_DIM:(2 * hh + 1) * V_DIM]))
        o_ref[:, cols] = _diff_finish(pv / denom, lam_ref[0], sg_ref[...], ds).astype(BF16)


def _attn_sample(lam, qa, qb, kb, vx, cache_k, cache_v, sub_gain, *, layer, n_heads, ds, row0):
    _, db, past, aw = cache_k.shape
    blk0 = row0 // ds
    new = lambda w: pl.BlockSpec((ds, w), lambda d: (blk0 + d, 0))
    cache = pl.BlockSpec((1, 1, past, aw), lambda d: (layer, d, 0, 0))
    vmem = 2 * (2 * past * aw * 4 + 6 * ds * aw * 2) + 8 * 2 * ds * past * 4 + 2 * past * V_DIM * 2
    return pl.pallas_call(
        functools.partial(_attn_sample_kernel, n_heads=n_heads, ds=ds),
        grid=(db,),
        in_specs=[pl.BlockSpec(memory_space=pltpu.SMEM), new(aw), new(aw), new(aw), new(2 * aw),
                  cache, cache, pl.BlockSpec((1, V_DIM), lambda d: (0, 0))],
        out_specs=pl.BlockSpec((ds, aw), lambda d: (d, 0)),
        out_shape=jax.ShapeDtypeStruct((db * ds, aw), BF16),
        compiler_params=_params(1, vmem + 8 * MIB),
        name="attn_sample",
    )(lam, qa, qb, kb, vx, cache_k, cache_v, sub_gain)


def _pool_conv_kernel(u_ref, g_ref, hu_ref, hg_ref, pw_ref, ps_ref, cw_ref, cb_ref, lg_ref, lb_ref, cpw_ref,
                      pool_ref, conv_ref, pu_sc, pg_sc, *, tm, tiles_per_seq, pos0, sub):
    i = pl.program_id(0)
    u = u_ref[...]
    pu_sc[0:POOL_HALO] = hu_ref[0]
    pu_sc[POOL_HALO:POOL_HALO + tm] = u
    pg_sc[0:CONV_HALO] = hg_ref[0]
    pg_sc[CONV_HALO:CONV_HALO + tm] = g_ref[...]

    padded = pu_sc[...]
    sums = []
    acc, width = padded, 1
    for win in POOL_WINDOWS:
        while width < win:
            acc = acc + pltpu.roll(acc, width, axis=0)
            width *= 2
        sums.append(acc[POOL_HALO:POOL_HALO + tm])
    c = u.shape[1]
    group = lax.broadcasted_iota(I32, (tm, c), 1) // (c // len(POOL_WINDOWS))
    pos = pos0 + (i % tiles_per_seq) * tm + lax.broadcasted_iota(I32, (tm, c), 0)
    tot, win_lane = sums[-1], jnp.full((tm, c), POOL_WINDOWS[-1], I32)
    for gi in range(len(POOL_WINDOWS) - 2, -1, -1):
        tot = jnp.where(group == gi, sums[gi], tot)
        win_lane = jnp.where(group == gi, POOL_WINDOWS[gi], win_lane)
    cnt = jnp.minimum(pos + 1, win_lane).astype(F32)
    d = tot / cnt - u
    pool_ref[...] = (_dot(d.astype(BF16), pw_ref[...]) * ps_ref[...]).astype(BF16)

    for r0 in range(0, tm, sub):
        y = jnp.zeros((sub, c), F32)
        for j in range(CONV_K):
            lo = r0 + CONV_HALO - (CONV_K - 1) + j
            y = y + pg_sc[lo:lo + sub, :] * cw_ref[j:j + 1, :]
        y = y + cb_ref[...]
        mu = jnp.mean(y, axis=-1, keepdims=True)
        yc = y - mu
        yn = yc * lax.rsqrt(jnp.mean(yc * yc, axis=-1, keepdims=True) + EPS) * lg_ref[...] + lb_ref[...]
        act = yn * jax.nn.sigmoid(yn)
        conv_ref[r0:r0 + sub, :] = _dot(act.astype(BF16), cpw_ref[...]).astype(BF16)


def _pool_conv(u, g, halo_u, halo_g, pool_wbd, pool_scale, conv_w, conv_b, ln_g, ln_b, conv_pw,
               *, row0, n_rows, tm, tiles_per_seq, pos0):
    c = u.shape[1]
    n_tiles = n_rows // tm
    blk0 = row0 // tm
    sub = _pick_tile(tm, (64, 32))
    row_in = pl.BlockSpec((tm, c), lambda i: (blk0 + i, 0))
    row_out = pl.BlockSpec((tm, c), lambda i: (i, 0))
    full = lambda a: pl.BlockSpec(a.shape, lambda i: (0,) * a.ndim)
    vmem = 2 * (2 * tm * c * 4 + 2 * tm * c * 2) + 14 * (tm + CONV_HALO) * c * 4 + 4 * c * c * 2
    return pl.pallas_call(
        functools.partial(_pool_conv_kernel, tm=tm, tiles_per_seq=tiles_per_seq, pos0=pos0, sub=sub),
        grid=(n_tiles,),
        in_specs=[row_in, row_in,
                  pl.BlockSpec((1, POOL_HALO, c), lambda i: (i, 0, 0)),
                  pl.BlockSpec((1, CONV_HALO, c), lambda i: (i, 0, 0)),
                  full(pool_wbd), full(pool_scale), full(conv_w), full(conv_b), full(ln_g), full(ln_b),
                  full(conv_pw)],
        out_specs=[row_out, row_out],
        out_shape=[jax.ShapeDtypeStruct((n_rows, c), BF16)] * 2,
        scratch_shapes=[pltpu.VMEM((tm + POOL_HALO, c), F32), pltpu.VMEM((tm + CONV_HALO, c), F32)],
        compiler_params=_params(1, vmem + 8 * MIB),
        name="pool_conv",
    )(u, g, halo_u, halo_g, pool_wbd, pool_scale, conv_w, conv_b, ln_g, ln_b, conv_pw)


def _tile_halos(rows, n_seq, seq_len, tm, halo, first):
    c = rows.shape[1]
    tiles = seq_len // tm
    tails = rows.reshape(n_seq, tiles, tm, c)[:, :tiles - 1, tm - halo:, :]
    return jnp.concatenate([first[:, None], tails], axis=1).reshape(n_seq * tiles, halo, c)


def _front_pad(state, halo):
    n, rows, c = state.shape
    return jnp.concatenate([jnp.zeros((n, halo - rows, c), state.dtype), state], axis=1)


def _mix_out(att_ref, pool_ref, conv_ref, x_ref, w_ref):
    aw, pw = att_ref.shape[1], pool_ref.shape[1]
    return (x_ref[...] + _dot(att_ref[...], w_ref[0:aw, :]) + _dot(pool_ref[...], w_ref[aw:aw + pw, :])
            + _dot(conv_ref[...], w_ref[aw + pw:, :]))


def _out_proj_dense_kernel(att_ref, pool_ref, conv_ref, x_ref, w_ref, fg_ref, xo_ref, h_ref):
    x = _mix_out(att_ref, pool_ref, conv_ref, x_ref, w_ref)
    xo_ref[...] = x
    h_ref[...] = _rms_rows(x, fg_ref[...]).astype(BF16)


def _out_proj_router_kernel(att_ref, pool_ref, conv_ref, x_ref, w_ref, fg_ref, rh_ref, rl_ref,
                            xo_ref, idx_ref, gate_ref):
    x = _mix_out(att_ref, pool_ref, conv_ref, x_ref, w_ref)
    xo_ref[...] = x
    h_hi, h_lo = _split_bf16(_rms_rows(x, fg_ref[...]))
    logits = _nt_dot(rh_ref[...], h_hi) + _nt_dot(rh_ref[...], h_lo) + _nt_dot(rl_ref[...], h_hi)
    n_exp = logits.shape[0]
    eidx = lax.broadcasted_iota(I32, logits.shape, 0)
    m1 = jnp.max(logits, axis=0, keepdims=True)
    i1 = jnp.min(jnp.where(logits == m1, eidx, n_exp), axis=0, keepdims=True)
    rest = jnp.where(eidx == i1, -jnp.inf, logits)
    m2 = jnp.max(rest, axis=0, keepdims=True)
    i2 = jnp.min(jnp.where(rest == m2, eidx, n_exp), axis=0, keepdims=True)
    e2 = jnp.exp(m2 - m1)
    idx_ref[...] = jnp.concatenate([i1, i2], axis=0)
    gate_ref[...] = jnp.concatenate([1.0 / (1.0 + e2), e2 / (1.0 + e2)], axis=0)


def _out_proj(att, pool, conv, x, w_out, ffn_g, router=None):
    t, d = x.shape
    tm = _pick_tile(t, (512, 256, 128))
    row = lambda a: pl.BlockSpec((tm, a.shape[1]), lambda i: (i, 0))
    full = lambda a: pl.BlockSpec(a.shape, lambda i: (0,) * a.ndim)
    ins = [att, pool, conv, x, w_out, ffn_g]
    in_specs = [row(att), row(pool), row(conv), row(x), full(w_out), full(ffn_g)]
    vmem = 2 * (tm * d * 2 + tm * d * 4 * 2 + d * d * 2 + tm * d * 2) + 6 * tm * d * 4
    if router is None:
        kern = _out_proj_dense_kernel
        out_specs = [row(x), row(x)]
        out_shape = [jax.ShapeDtypeStruct((t, d), F32), jax.ShapeDtypeStruct((t, d), BF16)]
    else:
        kern = _out_proj_router_kernel
        ins += list(router)
        in_specs += [full(router[0]), full(router[1])]
        top = pl.BlockSpec((TOP_K, tm), lambda i: (0, i))
        out_specs = [row(x), top, top]
        out_shape = [jax.ShapeDtypeStruct((t, d), F32), jax.ShapeDtypeStruct((TOP_K, t), I32),
                     jax.ShapeDtypeStruct((TOP_K, t), F32)]
    return pl.pallas_call(
        kern, grid=(t // tm,), in_specs=in_specs, out_specs=out_specs, out_shape=out_shape,
        compiler_params=_params(1, vmem + 8 * MIB), name="out_proj",
    )(*ins)


def _swiglu_acc(h, acc, wg_ref, wu_ref, wd_ref, lead, tf):
    f = wg_ref.shape[-1]
    for c0 in range(0, f, tf):
        gate = _dot(h, wg_ref[lead + (slice(None), slice(c0, c0 + tf))])
        up = _dot(h, wu_ref[lead + (slice(None), slice(c0, c0 + tf))])
        act = (gate * jax.nn.sigmoid(gate) * up).astype(BF16)
        acc = acc + _dot(act, wd_ref[lead + (slice(c0, c0 + tf), slice(None))])
    return acc


def _ffn_kernel(h_ref, x_ref, wg_ref, wu_ref, wd_ref, o_ref, *, tf):
    o_ref[...] = _swiglu_acc(h_ref[...], x_ref[...], wg_ref, wu_ref, wd_ref, (), tf)


def _ffn(h, x, wg, wu, wd):
    t, d = x.shape
    f = wg.shape[1]
    tm = _pick_tile(t, (512, 256, 128))
    tf = _pick_tile(f, (512, 256, 128))
    row = lambda: pl.BlockSpec((tm, d), lambda i: (i, 0))
    full = lambda a: pl.BlockSpec(a.shape, lambda i: (0,) * a.ndim)
    vmem = 2 * (3 * d * f * 2 + tm * d * (2 + 4 + 4)) + 4 * tm * tf * 4 + 2 * tm * d * 4
    return pl.pallas_call(
        functools.partial(_ffn_kernel, tf=tf),
        grid=(t // tm,),
        in_specs=[row(), row(), full(wg), full(wu), full(wd)],
        out_specs=row(),
        out_shape=jax.ShapeDtypeStruct((t, d), F32),
        compiler_params=_params(1, vmem + 4 * MIB),
        name="ffn",
    )(h, x, wg, wu, wd)


def _moe_kernel(te_ref, tv_ref, src_ref, src_next_ref, dst_ref, ws_ref, fg_ref, wg_ref, wu_ref, wd_ref, x_hbm,
                out_hbm, xbuf, ybuf, gsem, ssem, *, tm, tf, n_tiles):
    i = pl.program_id(0)
    slot = i % 2

    def gather_row(idx_ref, r, s):
        return pltpu.make_async_copy(x_hbm.at[pl.ds(idx_ref[0, 0, r], 1)], xbuf.at[s, pl.ds(r, 1)], gsem.at[s])

    def scatter_row(row, r, s):
        return pltpu.make_async_copy(ybuf.at[s, pl.ds(r, 1)], out_hbm.at[pl.ds(row, 1)], ssem.at[s])

    def for_rows(fn):
        def body(r, carry):
            fn(r)
            return carry
        lax.fori_loop(0, tm, body, 0, unroll=8)

    @pl.when(i == 0)
    def _():
        for_rows(lambda r: gather_row(src_ref, r, slot).start())

    @pl.when(i + 1 < n_tiles)
    def _():
        for_rows(lambda r: gather_row(src_next_ref, r, 1 - slot).start())

    for_rows(lambda r: gather_row(src_ref, r, slot).wait())

    @pl.when(i >= 2)
    def _():
        for_rows(lambda r: scatter_row(0, r, slot).wait())

    @pl.when(tv_ref[i] > 0)
    def _():
        h = _rms_rows(xbuf[slot], fg_ref[...]).astype(BF16)
        lead = (0,)
        y = _swiglu_acc(h, jnp.zeros((tm, h.shape[1]), F32), wg_ref, wu_ref, wd_ref, lead, tf)
        ybuf[slot] = y * ws_ref[...]

    @pl.when(tv_ref[i] == 0)
    def _():
        ybuf[slot] = jnp.zeros(ybuf.shape[1:], F32)

    for_rows(lambda r: scatter_row(dst_ref[0, 0, r], r, slot).start())

    @pl.when(i == n_tiles - 1)
    def _():
        for_rows(lambda r: scatter_row(0, r, slot).wait())
        if n_tiles > 1:
            for_rows(lambda r: scatter_row(0, r, 1 - slot).wait())


def _moe_route(idx, gate, tm, n_exp):
    k, t = idx.shape
    a = k * t
    e = idx.reshape(a)
    onehot = (e[:, None] == jnp.arange(n_exp, dtype=I32)[None, :]).astype(I32)
    csum = jnp.cumsum(onehot, axis=0)
    counts = csum[-1]
    rank = jnp.sum((csum - onehot) * onehot, axis=1)
    padded = ((counts + tm - 1) // tm) * tm
    ends = jnp.cumsum(padded)
    pos = jnp.sum(onehot * (ends - padded)[None, :], axis=1) + rank
    n_pad = a + n_exp * tm
    n_tiles = n_pad // tm
    slot_ids = jnp.arange(n_pad, dtype=I32)
    assign = jnp.arange(a, dtype=I32)
    src = jnp.zeros((n_pad,), I32).at[pos].set(assign % t)
    spare = a + ((slot_ids // tm) % 2) * tm + slot_ids % tm
    dst = spare.at[pos].set(assign)
    ws = jnp.zeros((n_pad,), F32).at[pos].set(gate.reshape(a))
    starts = jnp.arange(n_tiles, dtype=I32) * tm
    tile_expert = jnp.minimum(jnp.sum((starts[:, None] >= ends[None, :]).astype(I32), axis=1), n_exp - 1)
    tile_valid = (starts < ends[-1]).astype(I32)
    shape3 = (n_tiles, 1, tm)
    return tile_expert, tile_valid, src.reshape(shape3), dst.reshape(shape3), ws.reshape(n_pad, 1)


def _moe(x, idx, gate, ffn_g, wg, wu, wd):
    t, d = x.shape
    n_exp, _, f = wg.shape
    tm = 256
    tf = _pick_tile(f, (512, 256, 128))
    te, tv, src, dst, ws = _moe_route(idx, gate, tm, n_exp)
    n_tiles = src.shape[0]
    n_out = TOP_K * t + 2 * tm
    smem_rows = lambda nxt: pl.BlockSpec(
        (1, 1, tm), (lambda i, te, tv: (jnp.minimum(i + 1, n_tiles - 1), 0, 0)) if nxt else (lambda i, te, tv: (i, 0, 0)),
        memory_space=pltpu.SMEM)
    wspec = lambda a: pl.BlockSpec((1,) + a.shape[1:], lambda i, te, tv: (te[i], 0, 0))
    vmem = 2 * 3 * d * f * 2 + 4 * tm * d * 4 + 4 * tm * tf * 4 + 3 * tm * d * 4
    out2 = pl.pallas_call(
        functools.partial(_moe_kernel, tm=tm, tf=tf, n_tiles=n_tiles),
        grid_spec=pltpu.PrefetchScalarGridSpec(
            num_scalar_prefetch=2, grid=(n_tiles,),
            in_specs=[smem_rows(False), smem_rows(True), smem_rows(False),
                      pl.BlockSpec((tm, 1), lambda i, te, tv: (i, 0)),
                      pl.BlockSpec(ffn_g.shape, lambda i, te, tv: (0, 0)),
                      wspec(wg), wspec(wu), wspec(wd),
                      pl.BlockSpec(memory_space=pl.ANY)],
            out_specs=pl.BlockSpec(memory_space=pl.ANY),
            scratch_shapes=[pltpu.VMEM((2, tm, d), F32), pltpu.VMEM((2, tm, d), F32),
                            pltpu.SemaphoreType.DMA((2,)), pltpu.SemaphoreType.DMA((2,))]),
        out_shape=jax.ShapeDtypeStruct((n_out, d), F32),
        compiler_params=_params(1, vmem + 4 * MIB),
        name="moe_experts",
    )(te, tv, src, src, dst, ws, ffn_g, wg, wu, wd, x)

    tc = _pick_tile(t, (512, 256, 128))
    blocks = t // tc
    return pl.pallas_call(
        _moe_combine_kernel,
        grid=(blocks,),
        in_specs=[pl.BlockSpec((tc, d), lambda i: (i, 0))]
        + [pl.BlockSpec((tc, d), functools.partial(lambda i, kk: (kk * blocks + i, 0), kk=kk)) for kk in range(TOP_K)],
        out_specs=pl.BlockSpec((tc, d), lambda i: (i, 0)),
        out_shape=jax.ShapeDtypeStruct((t, d), F32),
        compiler_params=_params(1, 2 * 4 * tc * d * 4 + 4 * MIB),
        name="moe_combine",
    )(x, *([out2] * TOP_K))


def _moe_combine_kernel(x_ref, *refs):
    *y_refs, o_ref = refs
    acc = x_ref[...]
    for y_ref in y_refs:
        acc = acc + y_ref[...]
    o_ref[...] = acc


def kernel(x_prompt, x_sample, cache_k, cache_v, state_pool, state_conv, attn_norm_g, w_in, q_norm_g, k_norm_g,
           lambda_q1, lambda_k1, lambda_q2, lambda_k2, subln_g, pool_w, pool_scale, conv_w, conv_b, conv_ln_g,
           conv_ln_b, conv_pw, w_out, ffn_norm_g, ffn_w_gate, ffn_w_up, ffn_w_down, router_w, moe_w_gate,
           moe_w_up, moe_w_down):
    bp, sp, d = x_prompt.shape
    db, ds, _ = x_sample.shape
    depth = w_in.shape[0]
    past = cache_k.shape[2]
    n_sub = cache_k.shape[3]
    n_heads = n_sub // 2
    aw = n_heads * V_DIM
    pw = state_pool.shape[-1]
    cw = state_conv.shape[-1]
    pool_rows, conv_rows = state_pool.shape[2], state_conv.shape[2]
    tp, ts = bp * sp, db * ds
    assert pool_rows == max(POOL_WINDOWS) - 1 and conv_rows == CONV_K - 1
    assert w_in.shape[2] == 3 * aw + pw + 2 * cw and ds >= conv_rows and tp % ds == 0

    x = jnp.concatenate([x_prompt.reshape(tp, d), x_sample.reshape(ts, d)], axis=0)
    ck = cache_k.reshape(depth, db, past, aw)
    cv = cache_v.reshape(depth, db, past, aw)

    lane_group = jnp.arange(aw, dtype=I32) // HEAD_DIM
    gsum = (lane_group[:, None] == lane_group[None, :]).astype(BF16)
    n_groups = len(POOL_WINDOWS)
    pg = pw // n_groups
    tm_pool = _pick_tile(sp, (512, 256, 128))

    outs = {name: [] for name in ("kp", "vp", "pp", "cp", "ks", "vs", "ps", "cs")}
    for l in range(depth):
        lam_init = 0.8 - 0.6 * math.exp(-0.3 * l)
        lam = (jnp.exp(jnp.sum(lambda_q1[l] * lambda_k1[l])) - jnp.exp(jnp.sum(lambda_q2[l] * lambda_k2[l]))
               + lam_init).astype(F32).reshape(1)
        qg = (jnp.tile(q_norm_g[l], n_sub) * HEAD_DIM ** -0.5).reshape(1, aw)
        kg = jnp.tile(k_norm_g[l], n_sub).reshape(1, aw)
        sub_gain = (subln_g[l] * (1.0 - lam_init)).reshape(1, V_DIM)

        qa, qb, k32, kb, v32, vx, u, g = _in_proj(
            x, attn_norm_g[l].reshape(1, d), w_in[l].astype(BF16), gsum, qg, kg, aw=aw, pw=pw, cw=cw)

        att_p = _attn_prompt(lam, qa, qb, kb, vx, sub_gain, n_heads=n_heads, b=bp, s=sp)
        att_s = _attn_sample(lam, qa, qb, kb, vx, ck, cv, sub_gain, layer=l, n_heads=n_heads, ds=ds, row0=tp)
        att = jnp.concatenate([att_p, att_s], axis=0)

        pool_wbd = jax.scipy.linalg.block_diag(*[pool_w[l, gi] for gi in range(n_groups)]).astype(BF16)
        mix_w = (pool_wbd, pool_scale[l].reshape(1, pw), conv_w[l], conv_b[l].reshape(1, cw),
                 conv_ln_g[l].reshape(1, cw), conv_ln_b[l].reshape(1, cw), conv_pw[l].astype(BF16))
        zeros_p = lambda halo, c: jnp.zeros((bp, halo, c), F32)
        pool_p, conv_p = _pool_conv(
            u, g, _tile_halos(u[:tp], bp, sp, tm_pool, POOL_HALO, zeros_p(POOL_HALO, pw)),
            _tile_halos(g[:tp], bp, sp, tm_pool, CONV_HALO, zeros_p(CONV_HALO, cw)), *mix_w,
            row0=0, n_rows=tp, tm=tm_pool, tiles_per_seq=sp // tm_pool, pos0=0)
        pool_s, conv_s = _pool_conv(
            u, g, _front_pad(state_pool[l], POOL_HALO), _front_pad(state_conv[l], CONV_HALO), *mix_w,
            row0=tp, n_rows=ts, tm=ds, tiles_per_seq=1, pos0=past)
        pool = jnp.concatenate([pool_p, pool_s], axis=0)
        conv = jnp.concatenate([conv_p, conv_s], axis=0)

        w_o = w_out[l].astype(BF16)
        fg = ffn_norm_g[l].reshape(1, d)
        i = l // 2
        if l % 2 == 0:
            x, h = _out_proj(att, pool, conv, x, w_o, fg)
            x = _ffn(h, x, ffn_w_gate[i].astype(BF16), ffn_w_up[i].astype(BF16), ffn_w_down[i].astype(BF16))
        else:
            x, idx, gate = _out_proj(att, pool, conv, x, w_o, fg, router=_split_bf16(router_w[i].T))
            x = _moe(x, idx, gate, fg, moe_w_gate[i].astype(BF16), moe_w_up[i].astype(BF16),
                     moe_w_down[i].astype(BF16))

        outs["kp"].append(k32[:tp].reshape(bp, sp, n_sub, HEAD_DIM))
        outs["vp"].append(v32[:tp].reshape(bp, sp, n_heads, V_DIM))
        outs["pp"].append(u[:tp].reshape(bp, sp, pw)[:, sp - pool_rows:])
        outs["cp"].append(g[:tp].reshape(bp, sp, cw)[:, sp - conv_rows:])
        outs["ks"].append(k32[tp:].reshape(db, ds, n_sub, HEAD_DIM))
        outs["vs"].append(v32[tp:].reshape(db, ds, n_heads, V_DIM))
        outs["ps"].append(u[tp:].reshape(db, ds, pw)[:, ds - pool_rows:])
        outs["cs"].append(g[tp:].reshape(db, ds, cw)[:, ds - conv_rows:])

    stack = lambda name: jnp.stack(outs[name])
    return (x[:tp].reshape(bp, sp, d), x[tp:].reshape(db, ds, d),
            stack("kp"), stack("vp"), stack("pp"), stack("cp"),
            stack("ks"), stack("vs"), stack("ps"), stack("cs"))
```

```python
import functools
import math

import jax
import jax.numpy as jnp
from jax import lax
from jax.experimental import pallas as pl
from jax.experimental.pallas import tpu as pltpu

F32 = jnp.float32
BF16 = jnp.bfloat16
I32 = jnp.int32

EPS = 1e-6
CHUNK = 64
HEAD_DIM = 64
V_DIM = 2 * HEAD_DIM
POOL_WINDOWS = (2, 4, 8, 16)
POOL_HALO = 16
CONV_K = 31
CONV_HALO = 32
TOP_K = 2
NEG_BIG = -1e30

LANES = 128
MIB = 1024 * 1024
VMEM_CAP_BYTES = 60 * MIB


def _pick_tile(n, candidates):
    for c in candidates:
        if n % c == 0:
            return c
    raise ValueError(f"no tile in {candidates} divides {n}")


def _params(n_grid_dims, vmem_bytes):
    return pltpu.CompilerParams(dimension_semantics=("arbitrary",) * n_grid_dims,
                                vmem_limit_bytes=int(min(vmem_bytes, VMEM_CAP_BYTES)))


def _nt_dot(a, b):
    return lax.dot_general(a, b, (((1,), (1,)), ((), ())), preferred_element_type=F32)


def _dot(a, b):
    return jnp.dot(a, b, preferred_element_type=F32)


def _split_bf16(x):
    hi = x.astype(BF16)
    lo = (x - hi.astype(F32)).astype(BF16)
    return hi, lo


def _rms_rows(x, gain):
    ms = jnp.mean(x * x, axis=-1, keepdims=True)
    return x * lax.rsqrt(ms + EPS) * gain


def _in_proj_kernel(x_ref, ng_ref, w_ref, gsum_ref, qg_ref, kg_ref,
                    qa_ref, qb_ref, kb_ref, vx_ref, u_ref, g_ref, kp_ref, vp_ref, ks_ref, vs_ref,
                    *, aw, pw, cw, n_prompt_tiles):
    h = _rms_rows(x_ref[...], ng_ref[...]).astype(BF16)

    def proj(lo, width):
        return _dot(h, w_ref[:, lo:lo + width])

    def head_norm(t, gain):
        hi, lo = _split_bf16(t * t)
        ss = _dot(hi, gsum_ref[...]) + _dot(lo, gsum_ref[...])
        return t * lax.rsqrt(ss * (1.0 / HEAD_DIM) + EPS) * gain

    q = head_norm(proj(0, aw), qg_ref[...])
    lane = lax.broadcasted_iota(I32, q.shape, 1)
    first_map = (lane // HEAD_DIM) % 2 == 0
    qa_ref[...] = jnp.where(first_map, q, 0.0).astype(BF16)
    qb_ref[...] = jnp.where(first_map, 0.0, q).astype(BF16)

    k = head_norm(proj(aw, aw), kg_ref[...])
    kb_ref[...] = k.astype(BF16)
    v = proj(2 * aw, aw)
    vb = v.astype(BF16)
    ones = jnp.ones((vb.shape[0], V_DIM), BF16)
    for hh in range(aw // V_DIM):
        vx_ref[:, 2 * hh * V_DIM:(2 * hh + 1) * V_DIM] = vb[:, hh * V_DIM:(hh + 1) * V_DIM]
        vx_ref[:, (2 * hh + 1) * V_DIM:(2 * hh + 2) * V_DIM] = ones

    is_prompt = pl.program_id(0) < n_prompt_tiles

    @pl.when(is_prompt)
    def _():
        kp_ref[...] = k
        vp_ref[...] = v

    @pl.when(jnp.logical_not(is_prompt))
    def _():
        ks_ref[...] = k
        vs_ref[...] = v

    u_ref[...] = proj(3 * aw, pw)
    a = proj(3 * aw + pw, cw)
    gate = proj(3 * aw + pw + cw, cw)
    g_ref[...] = a * jax.nn.sigmoid(gate)


def _in_proj(x, norm_g, w_in, gsum, qg, kg, *, tp, aw, pw, cw):
    t, d = x.shape
    ts = t - tp
    tm = _pick_tile(math.gcd(tp, ts), (512, 256, 128))
    npt = tp // tm
    in_w = w_in.shape[1]
    row = lambda w: pl.BlockSpec((tm, w), lambda i: (i, 0))
    full = lambda a: pl.BlockSpec(a.shape, lambda i: (0,) * a.ndim)
    out_widths = (aw, aw, aw, 2 * aw, pw, cw)
    out_dtypes = (BF16, BF16, BF16, BF16, F32, F32)
    rows_p = pl.BlockSpec((tm, aw), lambda i: (jnp.minimum(i, npt - 1), 0))
    rows_s = pl.BlockSpec((tm, aw), lambda i: (jnp.maximum(i - npt, 0), 0))
    vmem = 2 * (tm * d * 4 + d * in_w * 2 + 4 * tm * aw * 4
                + sum(tm * w * jnp.dtype(dt).itemsize for w, dt in zip(out_widths, out_dtypes)))
    vmem += 8 * tm * aw * 4 + 2 * aw * aw * 2
    outs = pl.pallas_call(
        functools.partial(_in_proj_kernel, aw=aw, pw=pw, cw=cw, n_prompt_tiles=npt),
        grid=(t // tm,),
        in_specs=[row(d), full(norm_g), full(w_in), full(gsum), full(qg), full(kg)],
        out_specs=[row(w) for w in out_widths] + [rows_p, rows_p, rows_s, rows_s],
        out_shape=[jax.ShapeDtypeStruct((t, w), dt) for w, dt in zip(out_widths, out_dtypes)]
        + [jax.ShapeDtypeStruct((rows, aw), F32) for rows in (tp, tp, ts, ts)],
        compiler_params=_params(1, vmem + 8 * MIB),
        name="in_proj",
    )(x, norm_g, w_in, gsum, qg, kg)
    return outs[:len(out_widths)], outs[len(out_widths):]


def _diff_finish(o, lam, sub_gain, rows):
    a = o[:rows] - lam * o[rows:]
    return _rms_rows(a, sub_gain)


def _attn_prompt_kernel(lam_ref, qa_ref, qb_ref, k_ref, vx_ref, sg_ref, o_ref,
                        q_sc, s0_sc, s1_sc, p0_sc, p1_sc, a0_sc, a1_sc, m_sc, acc_sc, *, tq):
    i = pl.program_id(2)
    tk = tq
    q_sc[0:tq] = qa_ref[...]
    q_sc[tq:2 * tq] = qb_ref[...]
    m_sc[...] = jnp.full(m_sc.shape, NEG_BIG, F32)
    acc_sc[...] = jnp.zeros(acc_sc.shape, F32)
    p1_sc[...] = jnp.zeros(p1_sc.shape, BF16)
    a1_sc[...] = jnp.ones(a1_sc.shape, F32)
    even = (s0_sc, p0_sc, a0_sc)
    odd = (s1_sc, p1_sc, a1_sc)

    def rows(j):
        return pl.ds(pl.multiple_of(j * tk, tk), tk)

    def scores(j):
        return _nt_dot(q_sc[...], k_ref[rows(j), :])

    def softmax(buf, masked):
        s_ref, p_ref, a_ref = buf
        s = s_ref[...]
        if masked:
            q_chunk = (lax.broadcasted_iota(I32, s.shape, 0) % tq) // CHUNK
            k_chunk = lax.broadcasted_iota(I32, s.shape, 1) // CHUNK
            s = jnp.where(k_chunk <= q_chunk, s, NEG_BIG)
        m_old = m_sc[...]
        m_new = jnp.maximum(m_old, jnp.max(s, axis=-1, keepdims=True))
        a_ref[...] = jnp.exp2(m_old - m_new)
        p_ref[...] = jnp.exp2(s - jnp.tile(m_new, (1, tk // LANES))).astype(BF16)
        m_sc[...] = m_new

    def values(buf, j):
        _, p_ref, a_ref = buf
        acc_sc[...] = (acc_sc[...] * jnp.tile(a_ref[...], (1, 2 * V_DIM // LANES))
                       + _dot(p_ref[...], vx_ref[rows(j), :]))

    def step(cur, nxt, j):
        softmax(cur, masked=False)
        nxt[0][...] = scores(j + 1)
        values(nxt, jnp.maximum(j - 1, 0))

    def last(cur, prv):
        softmax(cur, masked=True)
        values(prv, jnp.maximum(i - 1, 0))
        values(cur, i)

    s0_sc[...] = scores(0)

    def pair(t, carry):
        step(even, odd, 2 * t)
        step(odd, even, 2 * t + 1)
        return carry

    lax.fori_loop(0, i // 2, pair, 0)

    @pl.when(i % 2 == 1)
    def _():
        step(even, odd, i - 1)
        last(odd, even)

    @pl.when(i % 2 == 0)
    def _():
        last(even, odd)

    acc = acc_sc[...]
    o = acc[:, :V_DIM] / acc[:, V_DIM:]
    o_ref[...] = _diff_finish(o, lam_ref[0], sg_ref[...], tq).astype(BF16)


def _attn_prompt(lam, qa, qb, kb, vx, sub_gain, *, n_heads, b, s):
    tq = _pick_tile(s, (512, 256, 128))
    tk = tq
    nq = s // tq
    qspec = pl.BlockSpec((tq, V_DIM), lambda bi, hi, qi: (bi * nq + qi, hi))
    score_buf = pltpu.VMEM((2 * tq, tk), F32)
    prob_buf = pltpu.VMEM((2 * tq, tk), BF16)
    row_buf = pltpu.VMEM((2 * tq, LANES), F32)
    resident = pl.Buffered(1)
    vmem = s * V_DIM * 2 + s * 2 * V_DIM * 2 + 2 * (2 * tq * V_DIM * 2 + tq * V_DIM * 2)
    vmem += 2 * tq * (V_DIM * 2 + 3 * LANES * 4 + 2 * V_DIM * 4) + (2 * 4 + 2 * 2 + 3 * 4) * 2 * tq * tk
    return pl.pallas_call(
        functools.partial(_attn_prompt_kernel, tq=tq),
        grid=(b, n_heads, nq),
        in_specs=[pl.BlockSpec(memory_space=pltpu.SMEM), qspec, qspec,
                  pl.BlockSpec((s, V_DIM), lambda bi, hi, qi: (bi, hi), pipeline_mode=resident),
                  pl.BlockSpec((s, 2 * V_DIM), lambda bi, hi, qi: (bi, hi), pipeline_mode=resident),
                  pl.BlockSpec((1, V_DIM), lambda bi, hi, qi: (0, 0))],
        out_specs=qspec,
        out_shape=jax.ShapeDtypeStruct((b * s, n_heads * V_DIM), BF16),
        scratch_shapes=[pltpu.VMEM((2 * tq, V_DIM), BF16), score_buf, score_buf, prob_buf, prob_buf,
                        row_buf, row_buf, row_buf, pltpu.VMEM((2 * tq, 2 * V_DIM), F32)],
        compiler_params=_params(3, vmem + 8 * MIB),
        name="attn_prompt",
    )(lam, qa, qb, kb, vx, sub_gain)


def _attn_sample_kernel(lam_ref, qa_ref, qb_ref, kn_ref, vn_ref, ck_ref, cv_ref, sg_ref, o_ref,
                        *, n_heads, ds):
    for hh in range(n_heads):
        cols = slice(hh * V_DIM, (hh + 1) * V_DIM)
        q2 = jnp.concatenate([qa_ref[:, cols], qb_ref[:, cols]], axis=0)
        s_past = _nt_dot(q2, ck_ref[0, 0, :, cols].astype(BF16))
        s_new = _nt_dot(q2, kn_ref[:, cols])
        m = jnp.maximum(jnp.max(s_past, axis=-1, keepdims=True),
                        jnp.max(s_new, axis=-1, keepdims=True))
        p_past = jnp.exp2(s_past - m)
        p_new = jnp.exp2(s_new - m)
        denom = jnp.sum(p_past, axis=-1, keepdims=True) + jnp.sum(p_new, axis=-1, keepdims=True)
        pv = (_dot(p_past.astype(BF16), cv_ref[0, 0, :, cols].astype(BF16))
              + _dot(p_new.astype(BF16), vn_ref[:, 2 * hh * V_DIM:(2 * hh + 1) * V_DIM]))
        o_ref[:, cols] = _diff_finish(pv / denom, lam_ref[0], sg_ref[...], ds).astype(BF16)


def _attn_sample(lam, qa, qb, kb, vx, cache_k, cache_v, sub_gain, *, layer, n_heads, ds, row0):
    _, db, past, aw = cache_k.shape
    blk0 = row0 // ds
    new = lambda w: pl.BlockSpec((ds, w), lambda d: (blk0 + d, 0))
    cache = pl.BlockSpec((1, 1, past, aw), lambda d: (layer, d, 0, 0))
    vmem = 2 * (2 * past * aw * 4 + 6 * ds * aw * 2) + 8 * 2 * ds * past * 4 + 2 * past * V_DIM * 2
    return pl.pallas_call(
        functools.partial(_attn_sample_kernel, n_heads=n_heads, ds=ds),
        grid=(db,),
        in_specs=[pl.BlockSpec(memory_space=pltpu.SMEM), new(aw), new(aw), new(aw), new(2 * aw),
                  cache, cache, pl.BlockSpec((1, V_DIM), lambda d: (0, 0))],
        out_specs=pl.BlockSpec((ds, aw), lambda d: (d, 0)),
        out_shape=jax.ShapeDtypeStruct((db * ds, aw), BF16),
        compiler_params=_params(1, vmem + 8 * MIB),
        name="attn_sample",
    )(lam, qa, qb, kb, vx, cache_k, cache_v, sub_gain)


def _pool_conv_kernel(u_ref, g_ref, hu_ref, hg_ref, pw_ref, ps_ref, cw_ref, cb_ref, lg_ref, lb_ref, cpw_ref,
                      pool_ref, conv_ref, pu_sc, pg_sc, *, tm, tiles_per_seq, pos0, sub):
    i = pl.program_id(0)
    u = u_ref[...]
    pu_sc[0:POOL_HALO] = hu_ref[0]
    pu_sc[POOL_HALO:POOL_HALO + tm] = u
    pg_sc[0:CONV_HALO] = hg_ref[0]
    pg_sc[CONV_HALO:CONV_HALO + tm] = g_ref[...]

    padded = pu_sc[...]
    sums = []
    acc, width = padded, 1
    for win in POOL_WINDOWS:
        while width < win:
            acc = acc + pltpu.roll(acc, width, axis=0)
            width *= 2
        sums.append(acc[POOL_HALO:POOL_HALO + tm])
    c = u.shape[1]
    group = lax.broadcasted_iota(I32, (tm, c), 1) // (c // len(POOL_WINDOWS))
    pos = pos0 + (i % tiles_per_seq) * tm + lax.broadcasted_iota(I32, (tm, c), 0)
    tot, win_lane = sums[-1], jnp.full((tm, c), POOL_WINDOWS[-1], I32)
    for gi in range(len(POOL_WINDOWS) - 2, -1, -1):
        tot = jnp.where(group == gi, sums[gi], tot)
        win_lane = jnp.where(group == gi, POOL_WINDOWS[gi], win_lane)
    cnt = jnp.minimum(pos + 1, win_lane).astype(F32)
    d = tot / cnt - u
    pool_ref[...] = (_dot(d.astype(BF16), pw_ref[...]) * ps_ref[...]).astype(BF16)

    for r0 in range(0, tm, sub):
        y = jnp.zeros((sub, c), F32)
        for j in range(CONV_K):
            lo = r0 + CONV_HALO - (CONV_K - 1) + j
            y = y + pg_sc[lo:lo + sub, :] * cw_ref[j:j + 1, :]
        y = y + cb_ref[...]
        mu = jnp.mean(y, axis=-1, keepdims=True)
        yc = y - mu
        yn = yc * lax.rsqrt(jnp.mean(yc * yc, axis=-1, keepdims=True) + EPS) * lg_ref[...] + lb_ref[...]
        act = yn * jax.nn.sigmoid(yn)
        conv_ref[r0:r0 + sub, :] = _dot(act.astype(BF16), cpw_ref[...]).astype(BF16)


def _pool_conv(u, g, halo_u, halo_g, pool_wbd, pool_scale, conv_w, conv_b, ln_g, ln_b, conv_pw,
               *, row0, n_rows, tm, tiles_per_seq, pos0):
    c = u.shape[1]
    n_tiles = n_rows // tm
    blk0 = row0 // tm
    sub = _pick_tile(tm, (64, 32))
    row_in = pl.BlockSpec((tm, c), lambda i: (blk0 + i, 0))
    row_out = pl.BlockSpec((tm, c), lambda i: (i, 0))
    full = lambda a: pl.BlockSpec(a.shape, lambda i: (0,) * a.ndim)
    vmem = 2 * (2 * tm * c * 4 + 2 * tm * c * 2) + 14 * (tm + CONV_HALO) * c * 4 + 4 * c * c * 2
    return pl.pallas_call(
        functools.partial(_pool_conv_kernel, tm=tm, tiles_per_seq=tiles_per_seq, pos0=pos0, sub=sub),
        grid=(n_tiles,),
        in_specs=[row_in, row_in,
                  pl.BlockSpec((1, POOL_HALO, c), lambda i: (i, 0, 0)),
                  pl.BlockSpec((1, CONV_HALO, c), lambda i: (i, 0, 0)),
                  full(pool_wbd), full(pool_scale), full(conv_w), full(conv_b), full(ln_g), full(ln_b),
                  full(conv_pw)],
        out_specs=[row_out, row_out],
        out_shape=[jax.ShapeDtypeStruct((n_rows, c), BF16)] * 2,
        scratch_shapes=[pltpu.VMEM((tm + POOL_HALO, c), F32), pltpu.VMEM((tm + CONV_HALO, c), F32)],
        compiler_params=_params(1, vmem + 8 * MIB),
        name="pool_conv",
    )(u, g, halo_u, halo_g, pool_wbd, pool_scale, conv_w, conv_b, ln_g, ln_b, conv_pw)


def _tile_halos(rows, n_seq, seq_len, tm, halo, first):
    c = rows.shape[1]
    tiles = seq_len // tm
    tails = rows.reshape(n_seq, tiles, tm, c)[:, :tiles - 1, tm - halo:, :]
    return jnp.concatenate([first[:, None], tails], axis=1).reshape(n_seq * tiles, halo, c)


def _front_pad(state, halo):
    n, rows, c = state.shape
    return jnp.concatenate([jnp.zeros((n, halo - rows, c), state.dtype), state], axis=1)


def _mix_out(att_ref, pool_ref, conv_ref, x_ref, w_ref):
    aw, pw = att_ref.shape[1], pool_ref.shape[1]
    return (x_ref[...] + _dot(att_ref[...], w_ref[0:aw, :]) + _dot(pool_ref[...], w_ref[aw:aw + pw, :])
            + _dot(conv_ref[...], w_ref[aw + pw:, :]))


def _out_proj_dense_kernel(att_ref, pool_ref, conv_ref, x_ref, w_ref, fg_ref, xo_ref, h_ref):
    x = _mix_out(att_ref, pool_ref, conv_ref, x_ref, w_ref)
    xo_ref[...] = x
    h_ref[...] = _rms_rows(x, fg_ref[...]).astype(BF16)


def _out_proj_router_kernel(att_ref, pool_ref, conv_ref, x_ref, w_ref, fg_ref, rh_ref, rl_ref, tri_ref,
                            xo_ref, idx_ref, gate_ref, rank_ref, cnt_ref, carry_sc):
    @pl.when(pl.program_id(0) == 0)
    def _():
        carry_sc[...] = jnp.zeros(carry_sc.shape, F32)

    x = _mix_out(att_ref, pool_ref, conv_ref, x_ref, w_ref)
    xo_ref[...] = x
    h_hi, h_lo = _split_bf16(_rms_rows(x, fg_ref[...]))
    logits = _nt_dot(rh_ref[...], h_hi) + _nt_dot(rh_ref[...], h_lo) + _nt_dot(rl_ref[...], h_hi)
    n_exp = logits.shape[0]
    eidx = lax.broadcasted_iota(I32, logits.shape, 0)
    m1 = jnp.max(logits, axis=0, keepdims=True)
    i1 = jnp.min(jnp.where(logits == m1, eidx, n_exp), axis=0, keepdims=True)
    rest = jnp.where(eidx == i1, -jnp.inf, logits)
    m2 = jnp.max(rest, axis=0, keepdims=True)
    i2 = jnp.min(jnp.where(rest == m2, eidx, n_exp), axis=0, keepdims=True)
    e2 = jnp.exp(m2 - m1)
    idx_ref[...] = jnp.concatenate([i1, i2], axis=0)
    gate_ref[...] = jnp.concatenate([1.0 / (1.0 + e2), e2 / (1.0 + e2)], axis=0)

    pick1, pick2 = eidx == i1, eidx == i2
    picks = jnp.concatenate([pick1, pick2], axis=0).astype(BF16)
    earlier = _dot(picks, tri_ref[...])
    tot1 = jnp.sum(pick1.astype(F32), axis=1, keepdims=True)
    tot2 = jnp.sum(pick2.astype(F32), axis=1, keepdims=True)
    carry = carry_sc[...]
    base = carry[:, 0:1]
    r1 = jnp.sum(jnp.where(pick1, base + earlier[:n_exp], 0.0), axis=0, keepdims=True)
    r2 = jnp.sum(jnp.where(pick2, base + tot1 + earlier[n_exp:], 0.0), axis=0, keepdims=True)
    rank_ref[...] = jnp.concatenate([r1, r2], axis=0).astype(I32)
    carry = carry + tot1 + tot2
    carry_sc[...] = carry
    cnt_ref[...] = carry.astype(I32)


def _out_proj(att, pool, conv, x, w_out, ffn_g, router=None):
    t, d = x.shape
    tm = _pick_tile(t, (512, 256, 128))
    row = lambda a: pl.BlockSpec((tm, a.shape[1]), lambda i: (i, 0))
    full = lambda a: pl.BlockSpec(a.shape, lambda i: (0,) * a.ndim)
    ins = [att, pool, conv, x, w_out, ffn_g]
    in_specs = [row(att), row(pool), row(conv), row(x), full(w_out), full(ffn_g)]
    vmem = 2 * (tm * d * 2 + tm * d * 4 * 2 + d * d * 2 + tm * d * 2) + 6 * tm * d * 4
    scratch = []
    if router is None:
        kern = _out_proj_dense_kernel
        out_specs = [row(x), row(x)]
        out_shape = [jax.ShapeDtypeStruct((t, d), F32), jax.ShapeDtypeStruct((t, d), BF16)]
    else:
        kern = _out_proj_router_kernel
        n_exp = router[0].shape[0]
        earlier_token = jnp.arange(tm, dtype=I32)[:, None] < jnp.arange(tm, dtype=I32)[None, :]
        ins += list(router) + [earlier_token.astype(BF16)]
        in_specs += [full(router[0]), full(router[1]), pl.BlockSpec((tm, tm), lambda i: (0, 0))]
        top = pl.BlockSpec((TOP_K, tm), lambda i: (0, i))
        out_specs = [row(x), top, top, top, pl.BlockSpec((n_exp, LANES), lambda i: (0, 0))]
        out_shape = [jax.ShapeDtypeStruct((t, d), F32), jax.ShapeDtypeStruct((TOP_K, t), I32),
                     jax.ShapeDtypeStruct((TOP_K, t), F32), jax.ShapeDtypeStruct((TOP_K, t), I32),
                     jax.ShapeDtypeStruct((n_exp, LANES), I32)]
        scratch = [pltpu.VMEM((n_exp, LANES), F32)]
        vmem += 2 * tm * tm * 2
    return pl.pallas_call(
        kern, grid=(t // tm,), in_specs=in_specs, out_specs=out_specs, out_shape=out_shape,
        scratch_shapes=scratch, compiler_params=_params(1, vmem + 8 * MIB), name="out_proj",
    )(*ins)


def _swiglu_acc(h, acc, wg_ref, wu_ref, wd_ref, lead, tf):
    f = wg_ref.shape[-1]
    for c0 in range(0, f, tf):
        gate = _dot(h, wg_ref[lead + (slice(None), slice(c0, c0 + tf))])
        up = _dot(h, wu_ref[lead + (slice(None), slice(c0, c0 + tf))])
        act = (gate * jax.nn.sigmoid(gate) * up).astype(BF16)
        acc = acc + _dot(act, wd_ref[lead + (slice(c0, c0 + tf), slice(None))])
    return acc


def _ffn_kernel(h_ref, x_ref, wg_ref, wu_ref, wd_ref, o_ref, *, tf):
    o_ref[...] = _swiglu_acc(h_ref[...], x_ref[...], wg_ref, wu_ref, wd_ref, (), tf)


def _ffn(h, x, wg, wu, wd):
    t, d = x.shape
    f = wg.shape[1]
    tm = _pick_tile(t, (512, 256, 128))
    tf = _pick_tile(f, (512, 256, 128))
    row = lambda: pl.BlockSpec((tm, d), lambda i: (i, 0))
    full = lambda a: pl.BlockSpec(a.shape, lambda i: (0,) * a.ndim)
    vmem = 2 * (3 * d * f * 2 + tm * d * (2 + 4 + 4)) + 4 * tm * tf * 4 + 2 * tm * d * 4
    return pl.pallas_call(
        functools.partial(_ffn_kernel, tf=tf),
        grid=(t // tm,),
        in_specs=[row(), row(), full(wg), full(wu), full(wd)],
        out_specs=row(),
        out_shape=jax.ShapeDtypeStruct((t, d), F32),
        compiler_params=_params(1, vmem + 4 * MIB),
        name="ffn",
    )(h, x, wg, wu, wd)


def _moe_kernel(te_ref, tv_ref, src_ref, src_next_ref, dst_ref, fg_ref, wg_ref, wu_ref, wd_ref, x_hbm,
                out_hbm, xbuf, ybuf, gsem, ssem, *, tm, tf, n_tiles):
    i = pl.program_id(0)
    slot = i % 2

    def gather_row(idx_ref, r, s):
        return pltpu.make_async_copy(x_hbm.at[pl.ds(idx_ref[0, 0, r], 1)], xbuf.at[s, pl.ds(r, 1)], gsem.at[s])

    def scatter_row(row, r, s):
        return pltpu.make_async_copy(ybuf.at[s, pl.ds(r, 1)], out_hbm.at[pl.ds(row, 1)], ssem.at[s])

    def for_rows(fn, static=False):
        if static:
            for r in range(tm):
                fn(r)
            return

        def body(r, carry):
            fn(r)
            return carry
        lax.fori_loop(0, tm, body, 0, unroll=8)

    @pl.when(i == 0)
    def _():
        for_rows(lambda r: gather_row(src_ref, r, slot).start())

    @pl.when(i + 1 < n_tiles)
    def _():
        for_rows(lambda r: gather_row(src_next_ref, r, 1 - slot).start(), static=True)

    for_rows(lambda r: gather_row(src_ref, r, slot).wait())

    @pl.when(i >= 2)
    def _():
        for_rows(lambda r: scatter_row(0, r, slot).wait())

    @pl.when(tv_ref[i] > 0)
    def _():
        h = _rms_rows(xbuf[slot], fg_ref[...]).astype(BF16)
        lead = (0,)
        ybuf[slot] = _swiglu_acc(h, jnp.zeros((tm, h.shape[1]), F32), wg_ref, wu_ref, wd_ref, lead, tf)

    @pl.when(tv_ref[i] == 0)
    def _():
        ybuf[slot] = jnp.zeros(ybuf.shape[1:], F32)

    for_rows(lambda r: scatter_row(dst_ref[0, 0, r], r, slot).start(), static=True)

    @pl.when(i == n_tiles - 1)
    def _():
        for_rows(lambda r: scatter_row(0, r, slot).wait())
        if n_tiles > 1:
            for_rows(lambda r: scatter_row(0, r, 1 - slot).wait())


def _moe_route(idx, rank, counts, tm):
    k, t = idx.shape
    a = k * t
    n_exp = counts.shape[0]
    padded = ((counts + tm - 1) // tm) * tm
    ends = jnp.cumsum(padded)
    offs = ends - padded
    pos = rank
    for e in range(n_exp):
        pos = pos + jnp.where(idx == e, offs[e], 0)
    n_pad = a + n_exp * tm
    n_tiles = n_pad // tm
    slot_ids = jnp.arange(n_pad, dtype=I32)
    choice = jnp.full((n_pad,), -1, I32).at[pos.reshape(a)].set(jnp.arange(a, dtype=I32))
    filled = choice >= 0
    src = jnp.where(filled, choice % t, 0)
    spare = a + ((slot_ids // tm) % 2) * tm + slot_ids % tm
    dst = jnp.where(filled, choice, spare)
    starts = jnp.arange(n_tiles, dtype=I32) * tm
    tile_expert = jnp.minimum(jnp.sum((starts[:, None] >= ends[None, :]).astype(I32), axis=1), n_exp - 1)
    tile_valid = (starts < ends[-1]).astype(I32)
    shape3 = (n_tiles, 1, tm)
    return tile_expert, tile_valid, src.reshape(shape3), dst.reshape(shape3)


def _moe(x, idx, gate, rank, counts, ffn_g, wg, wu, wd):
    t, d = x.shape
    n_exp, _, f = wg.shape
    tm = 256
    tf = _pick_tile(f, (512, 256, 128))
    te, tv, src, dst = _moe_route(idx, rank, counts, tm)
    n_tiles = src.shape[0]
    n_out = TOP_K * t + 2 * tm
    smem_rows = lambda nxt: pl.BlockSpec(
        (1, 1, tm), (lambda i, te, tv: (jnp.minimum(i + 1, n_tiles - 1), 0, 0)) if nxt else (lambda i, te, tv: (i, 0, 0)),
        memory_space=pltpu.SMEM)
    wspec = lambda a: pl.BlockSpec((1,) + a.shape[1:], lambda i, te, tv: (te[i], 0, 0))
    vmem = 2 * 3 * d * f * 2 + 4 * tm * d * 4 + 4 * tm * tf * 4 + 3 * tm * d * 4
    out2 = pl.pallas_call(
        functools.partial(_moe_kernel, tm=tm, tf=tf, n_tiles=n_tiles),
        grid_spec=pltpu.PrefetchScalarGridSpec(
            num_scalar_prefetch=2, grid=(n_tiles,),
            in_specs=[smem_rows(False), smem_rows(True), smem_rows(False),
                      pl.BlockSpec(ffn_g.shape, lambda i, te, tv: (0, 0)),
                      wspec(wg), wspec(wu), wspec(wd),
                      pl.BlockSpec(memory_space=pl.ANY)],
            out_specs=pl.BlockSpec(memory_space=pl.ANY),
            scratch_shapes=[pltpu.VMEM((2, tm, d), F32), pltpu.VMEM((2, tm, d), F32),
                            pltpu.SemaphoreType.DMA((2,)), pltpu.SemaphoreType.DMA((2,))]),
        out_shape=jax.ShapeDtypeStruct((n_out, d), F32),
        compiler_params=_params(1, vmem + 4 * MIB),
        name="moe_experts",
    )(te, tv, src, src, dst, ffn_g, wg, wu, wd, x)

    tc = _pick_tile(t, (512, 256, 128))
    blocks = t // tc
    return pl.pallas_call(
        _moe_combine_kernel,
        grid=(blocks,),
        in_specs=[pl.BlockSpec((tc, d), lambda i: (i, 0)), pl.BlockSpec((tc, TOP_K), lambda i: (i, 0))]
        + [pl.BlockSpec((tc, d), functools.partial(lambda i, kk: (kk * blocks + i, 0), kk=kk)) for kk in range(TOP_K)],
        out_specs=pl.BlockSpec((tc, d), lambda i: (i, 0)),
        out_shape=jax.ShapeDtypeStruct((t, d), F32),
        compiler_params=_params(1, 2 * 4 * tc * d * 4 + 4 * MIB),
        name="moe_combine",
    )(x, gate.T, *([out2] * TOP_K))


def _moe_combine_kernel(x_ref, g_ref, *refs):
    *y_refs, o_ref = refs
    acc = x_ref[...]
    g = g_ref[...]
    for kk, y_ref in enumerate(y_refs):
        acc = acc + g[:, kk:kk + 1] * y_ref[...]
    o_ref[...] = acc


def kernel(x_prompt, x_sample, cache_k, cache_v, state_pool, state_conv, attn_norm_g, w_in, q_norm_g, k_norm_g,
           lambda_q1, lambda_k1, lambda_q2, lambda_k2, subln_g, pool_w, pool_scale, conv_w, conv_b, conv_ln_g,
           conv_ln_b, conv_pw, w_out, ffn_norm_g, ffn_w_gate, ffn_w_up, ffn_w_down, router_w, moe_w_gate,
           moe_w_up, moe_w_down):
    bp, sp, d = x_prompt.shape
    db, ds, _ = x_sample.shape
    depth = w_in.shape[0]
    past = cache_k.shape[2]
    n_sub = cache_k.shape[3]
    n_heads = n_sub // 2
    aw = n_heads * V_DIM
    pw = state_pool.shape[-1]
    cw = state_conv.shape[-1]
    pool_rows, conv_rows = state_pool.shape[2], state_conv.shape[2]
    tp, ts = bp * sp, db * ds
    assert pool_rows == max(POOL_WINDOWS) - 1 and conv_rows == CONV_K - 1
    assert w_in.shape[2] == 3 * aw + pw + 2 * cw and ds >= conv_rows and tp % ds == 0

    x = jnp.concatenate([x_prompt.reshape(tp, d), x_sample.reshape(ts, d)], axis=0)
    ck = cache_k.reshape(depth, db, past, aw)
    cv = cache_v.reshape(depth, db, past, aw)

    lane_group = jnp.arange(aw, dtype=I32) // HEAD_DIM
    gsum = (lane_group[:, None] == lane_group[None, :]).astype(BF16)
    n_groups = len(POOL_WINDOWS)
    pg = pw // n_groups
    tm_pool = _pick_tile(sp, (512, 256, 128))

    outs = {name: [] for name in ("kp", "vp", "pp", "cp", "ks", "vs", "ps", "cs")}
    for l in range(depth):
        lam_init = 0.8 - 0.6 * math.exp(-0.3 * l)
        lam = (jnp.exp(jnp.sum(lambda_q1[l] * lambda_k1[l])) - jnp.exp(jnp.sum(lambda_q2[l] * lambda_k2[l]))
               + lam_init).astype(F32).reshape(1)
        qg = (jnp.tile(q_norm_g[l], n_sub) * (HEAD_DIM ** -0.5 * math.log2(math.e))).reshape(1, aw)
        kg = jnp.tile(k_norm_g[l], n_sub).reshape(1, aw)
        sub_gain = (subln_g[l] * (1.0 - lam_init)).reshape(1, V_DIM)

        (qa, qb, kb, vx, u, g), (kp, vp, ks, vs) = _in_proj(
            x, attn_norm_g[l].reshape(1, d), w_in[l].astype(BF16), gsum, qg, kg, tp=tp, aw=aw, pw=pw, cw=cw)

        att_p = _attn_prompt(lam, qa, qb, kb, vx, sub_gain, n_heads=n_heads, b=bp, s=sp)
        att_s = _attn_sample(lam, qa, qb, kb, vx, ck, cv, sub_gain, layer=l, n_heads=n_heads, ds=ds, row0=tp)
        att = jnp.concatenate([att_p, att_s], axis=0)

        pool_wbd = jax.scipy.linalg.block_diag(*[pool_w[l, gi] for gi in range(n_groups)]).astype(BF16)
        mix_w = (pool_wbd, pool_scale[l].reshape(1, pw), conv_w[l], conv_b[l].reshape(1, cw),
                 conv_ln_g[l].reshape(1, cw), conv_ln_b[l].reshape(1, cw), conv_pw[l].astype(BF16))
        zeros_p = lambda halo, c: jnp.zeros((bp, halo, c), F32)
        pool_p, conv_p = _pool_conv(
            u, g, _tile_halos(u[:tp], bp, sp, tm_pool, POOL_HALO, zeros_p(POOL_HALO, pw)),
            _tile_halos(g[:tp], bp, sp, tm_pool, CONV_HALO, zeros_p(CONV_HALO, cw)), *mix_w,
            row0=0, n_rows=tp, tm=tm_pool, tiles_per_seq=sp // tm_pool, pos0=0)
        pool_s, conv_s = _pool_conv(
            u, g, _front_pad(state_pool[l], POOL_HALO), _front_pad(state_conv[l], CONV_HALO), *mix_w,
            row0=tp, n_rows=ts, tm=ds, tiles_per_seq=1, pos0=past)
        pool = jnp.concatenate([pool_p, pool_s], axis=0)
        conv = jnp.concatenate([conv_p, conv_s], axis=0)

        w_o = w_out[l].astype(BF16)
        fg = ffn_norm_g[l].reshape(1, d)
        i = l // 2
        if l % 2 == 0:
            x, h = _out_proj(att, pool, conv, x, w_o, fg)
            x = _ffn(h, x, ffn_w_gate[i].astype(BF16), ffn_w_up[i].astype(BF16), ffn_w_down[i].astype(BF16))
        else:
            x, idx, gate, rank, counts = _out_proj(att, pool, conv, x, w_o, fg, router=_split_bf16(router_w[i].T))
            x = _moe(x, idx, gate, rank, counts[:, 0], fg, moe_w_gate[i].astype(BF16), moe_w_up[i].astype(BF16),
                     moe_w_down[i].astype(BF16))

        outs["kp"].append(kp.reshape(bp, sp, n_sub, HEAD_DIM))
        outs["vp"].append(vp.reshape(bp, sp, n_heads, V_DIM))
        outs["pp"].append(u[:tp].reshape(bp, sp, pw)[:, sp - pool_rows:])
        outs["cp"].append(g[:tp].reshape(bp, sp, cw)[:, sp - conv_rows:])
        outs["ks"].append(ks.reshape(db, ds, n_sub, HEAD_DIM))
        outs["vs"].append(vs.reshape(db, ds, n_heads, V_DIM))
        outs["ps"].append(u[tp:].reshape(db, ds, pw)[:, ds - pool_rows:])
        outs["cs"].append(g[tp:].reshape(db, ds, cw)[:, ds - conv_rows:])

    stack = lambda name: jnp.stack(outs[name])
    return (x[:tp].reshape(bp, sp, d), x[tp:].reshape(db, ds, d),
            stack("kp"), stack("vp"), stack("pp"), stack("cp"),
            stack("ks"), stack("vs"), stack("ps"), stack("cs"))
```

```python
import functools
import math

import jax
import jax.numpy as jnp
from jax import lax
from jax.experimental import pallas as pl
from jax.experimental.pallas import tpu as pltpu

F32 = jnp.float32
BF16 = jnp.bfloat16
I32 = jnp.int32

EPS = 1e-6
CHUNK = 64
HEAD_DIM = 64
V_DIM = 2 * HEAD_DIM
POOL_WINDOWS = (2, 4, 8, 16)
POOL_HALO = 16
CONV_K = 31
CONV_HALO = 32
TOP_K = 2
NEG_BIG = -1e30

LANES = 128
MIB = 1024 * 1024
VMEM_CAP_BYTES = 60 * MIB


def _pick_tile(n, candidates):
    for c in candidates:
        if n % c == 0:
            return c
    raise ValueError(f"no tile in {candidates} divides {n}")


def _params(n_grid_dims, vmem_bytes):
    return pltpu.CompilerParams(dimension_semantics=("arbitrary",) * n_grid_dims,
                                vmem_limit_bytes=int(min(vmem_bytes, VMEM_CAP_BYTES)))


def _nt_dot(a, b):
    return lax.dot_general(a, b, (((1,), (1,)), ((), ())), preferred_element_type=F32)


def _dot(a, b):
    return jnp.dot(a, b, preferred_element_type=F32)


def _split_bf16(x):
    hi = x.astype(BF16)
    lo = (x - hi.astype(F32)).astype(BF16)
    return hi, lo


def _rms_rows(x, gain):
    ms = jnp.mean(x * x, axis=-1, keepdims=True)
    return x * lax.rsqrt(ms + EPS) * gain


def _in_proj_kernel(x_ref, ng_ref, w_ref, gsum_ref, qg_ref, kg_ref,
                    qa_ref, qb_ref, kb_ref, vx_ref, u_ref, g_ref, kp_ref, vp_ref, ks_ref, vs_ref,
                    *, aw, pw, cw, n_prompt_tiles):
    h = _rms_rows(x_ref[...], ng_ref[...]).astype(BF16)

    def proj(lo, width):
        return _dot(h, w_ref[:, lo:lo + width])

    def head_norm(t, gain):
        hi, lo = _split_bf16(t * t)
        ss = _dot(hi, gsum_ref[...]) + _dot(lo, gsum_ref[...])
        return t * lax.rsqrt(ss * (1.0 / HEAD_DIM) + EPS) * gain

    q = head_norm(proj(0, aw), qg_ref[...])
    lane = lax.broadcasted_iota(I32, q.shape, 1)
    first_map = (lane // HEAD_DIM) % 2 == 0
    qa_ref[...] = jnp.where(first_map, q, 0.0).astype(BF16)
    qb_ref[...] = jnp.where(first_map, 0.0, q).astype(BF16)

    k = head_norm(proj(aw, aw), kg_ref[...])
    kb_ref[...] = k.astype(BF16)
    v = proj(2 * aw, aw)
    vb = v.astype(BF16)
    ones = jnp.ones((vb.shape[0], V_DIM), BF16)
    for hh in range(aw // V_DIM):
        vx_ref[:, 2 * hh * V_DIM:(2 * hh + 1) * V_DIM] = vb[:, hh * V_DIM:(hh + 1) * V_DIM]
        vx_ref[:, (2 * hh + 1) * V_DIM:(2 * hh + 2) * V_DIM] = ones

    is_prompt = pl.program_id(0) < n_prompt_tiles

    @pl.when(is_prompt)
    def _():
        kp_ref[...] = k
        vp_ref[...] = v

    @pl.when(jnp.logical_not(is_prompt))
    def _():
        ks_ref[...] = k
        vs_ref[...] = v

    u_ref[...] = proj(3 * aw, pw)
    a = proj(3 * aw + pw, cw)
    gate = proj(3 * aw + pw + cw, cw)
    g_ref[...] = a * jax.nn.sigmoid(gate)


def _in_proj(x, norm_g, w_in, gsum, qg, kg, *, tp, aw, pw, cw):
    t, d = x.shape
    ts = t - tp
    tm = _pick_tile(math.gcd(tp, ts), (512, 256, 128))
    npt = tp // tm
    in_w = w_in.shape[1]
    row = lambda w: pl.BlockSpec((tm, w), lambda i: (i, 0))
    full = lambda a: pl.BlockSpec(a.shape, lambda i: (0,) * a.ndim)
    out_widths = (aw, aw, aw, 2 * aw, pw, cw)
    out_dtypes = (BF16, BF16, BF16, BF16, F32, F32)
    rows_p = pl.BlockSpec((tm, aw), lambda i: (jnp.minimum(i, npt - 1), 0))
    rows_s = pl.BlockSpec((tm, aw), lambda i: (jnp.maximum(i - npt, 0), 0))
    vmem = 2 * (tm * d * 4 + d * in_w * 2 + 4 * tm * aw * 4
                + sum(tm * w * jnp.dtype(dt).itemsize for w, dt in zip(out_widths, out_dtypes)))
    vmem += 8 * tm * aw * 4 + 2 * aw * aw * 2
    outs = pl.pallas_call(
        functools.partial(_in_proj_kernel, aw=aw, pw=pw, cw=cw, n_prompt_tiles=npt),
        grid=(t // tm,),
        in_specs=[row(d), full(norm_g), full(w_in), full(gsum), full(qg), full(kg)],
        out_specs=[row(w) for w in out_widths] + [rows_p, rows_p, rows_s, rows_s],
        out_shape=[jax.ShapeDtypeStruct((t, w), dt) for w, dt in zip(out_widths, out_dtypes)]
        + [jax.ShapeDtypeStruct((rows, aw), F32) for rows in (tp, tp, ts, ts)],
        compiler_params=_params(1, vmem + 8 * MIB),
        name="in_proj",
    )(x, norm_g, w_in, gsum, qg, kg)
    return outs[:len(out_widths)], outs[len(out_widths):]


def _diff_finish(o, lam, sub_gain, rows):
    a = o[:rows] - lam * o[rows:]
    return _rms_rows(a, sub_gain)


def _attn_prompt_kernel(lam_ref, qa_ref, qb_ref, k_ref, vx_ref, sg_ref, o_ref,
                        q_sc, s0_sc, s1_sc, p0_sc, p1_sc, a0_sc, a1_sc, m_sc, acc_sc, *, tq):
    i = pl.program_id(2)
    tk = tq
    q_sc[0:tq] = qa_ref[...]
    q_sc[tq:2 * tq] = qb_ref[...]
    m_sc[...] = jnp.full(m_sc.shape, NEG_BIG, F32)
    acc_sc[...] = jnp.zeros(acc_sc.shape, F32)
    p1_sc[...] = jnp.zeros(p1_sc.shape, BF16)
    a1_sc[...] = jnp.ones(a1_sc.shape, F32)
    even = (s0_sc, p0_sc, a0_sc)
    odd = (s1_sc, p1_sc, a1_sc)

    def rows(j):
        return pl.ds(pl.multiple_of(j * tk, tk), tk)

    def scores(j):
        return _nt_dot(q_sc[...], k_ref[rows(j), :])

    def softmax(buf, masked):
        s_ref, p_ref, a_ref = buf
        s = s_ref[...]
        if masked:
            q_chunk = (lax.broadcasted_iota(I32, s.shape, 0) % tq) // CHUNK
            k_chunk = lax.broadcasted_iota(I32, s.shape, 1) // CHUNK
            s = jnp.where(k_chunk <= q_chunk, s, NEG_BIG)
        m_old = m_sc[...]
        m_new = jnp.maximum(m_old, jnp.max(s, axis=-1, keepdims=True))
        a_ref[...] = jnp.exp2(m_old - m_new)
        p_ref[...] = jnp.exp2(s - jnp.tile(m_new, (1, tk // LANES))).astype(BF16)
        m_sc[...] = m_new

    def values(buf, j):
        _, p_ref, a_ref = buf
        acc_sc[...] = (acc_sc[...] * jnp.tile(a_ref[...], (1, 2 * V_DIM // LANES))
                       + _dot(p_ref[...], vx_ref[rows(j), :]))

    def step(cur, nxt, j):
        softmax(cur, masked=False)
        nxt[0][...] = scores(j + 1)
        values(nxt, jnp.maximum(j - 1, 0))

    def by_parity(n, fn):
        pl.when(n % 2 == 0)(lambda: fn(even, odd))
        pl.when(n % 2 == 1)(lambda: fn(odd, even))

    s0_sc[...] = scores(0)

    def body(j, carry):
        by_parity(j, lambda cur, nxt: step(cur, nxt, j))
        return carry

    lax.fori_loop(0, i, body, 0)

    def boundary_tile(cur, prv):
        softmax(cur, masked=True)
        values(prv, jnp.maximum(i - 1, 0))

    by_parity(i, boundary_tile)
    by_parity(i, lambda cur, prv: values(cur, i))

    acc = acc_sc[...]
    o = acc[:, :V_DIM] / acc[:, V_DIM:]
    o_ref[...] = _diff_finish(o, lam_ref[0], sg_ref[...], tq).astype(BF16)


def _attn_prompt_bounded_kernel(lam_ref, qa_ref, qb_ref, k_ref, vx_ref, sg_ref, o_ref,
                                q_sc, p0_sc, p1_sc, acc_sc, *, tq):
    i = pl.program_id(2)
    q_sc[0:tq] = qa_ref[...]
    q_sc[tq:2 * tq] = qb_ref[...]
    acc_sc[...] = jnp.zeros(acc_sc.shape, F32)

    def rows(j):
        return pl.ds(pl.multiple_of(j * tq, tq), tq)

    def probs(j, boundary=False):
        s = _nt_dot(q_sc[...], k_ref[rows(j), :])
        if boundary:
            q_chunk = (lax.broadcasted_iota(I32, s.shape, 0) % tq) // CHUNK
            k_chunk = lax.broadcasted_iota(I32, s.shape, 1) // CHUNK
            s = jnp.where(k_chunk <= q_chunk, s, NEG_BIG)
        return jnp.exp2(s).astype(BF16)

    def values(p_ref, j):
        acc_sc[...] += _dot(p_ref[...], vx_ref[rows(j), :])

    def by_parity(n, fn, when=True):
        pl.when(jnp.logical_and(when, n % 2 == 0))(lambda: fn(p0_sc, p1_sc))
        pl.when(jnp.logical_and(when, n % 2 == 1))(lambda: fn(p1_sc, p0_sc))

    def first_tile(boundary):
        p0_sc[...] = probs(0, boundary)

    pl.when(i == 0)(lambda: first_tile(True))
    pl.when(i > 0)(lambda: first_tile(False))

    def step(cur, nxt, j):
        nxt[...] = probs(j + 1)
        values(cur, j)

    def body(j, carry):
        by_parity(j, lambda cur, nxt: step(cur, nxt, j))
        return carry

    lax.fori_loop(0, jnp.maximum(i - 1, 0), body, 0)

    def boundary_tile(cur, prv):
        cur[...] = probs(i, boundary=True)
        values(prv, i - 1)

    by_parity(i, boundary_tile, when=i > 0)
    by_parity(i, lambda cur, prv: values(cur, i))

    acc = acc_sc[...]
    o = acc[:, :V_DIM] / acc[:, V_DIM:]
    o_ref[...] = _diff_finish(o, lam_ref[0], sg_ref[...], tq).astype(BF16)


SCORE_BOUND_FOR_PLAIN_EXP2 = 64.0


def _attn_prompt(lam, qa, qb, kb, vx, sub_gain, score_bound, *, n_heads, b, s):
    tq = _pick_tile(s, (512, 256, 128))
    tk = tq
    nq = s // tq
    qspec = pl.BlockSpec((tq, V_DIM), lambda bi, hi, qi: (bi * nq + qi, hi))
    score_buf = pltpu.VMEM((2 * tq, tk), F32)
    prob_buf = pltpu.VMEM((2 * tq, tk), BF16)
    row_buf = pltpu.VMEM((2 * tq, LANES), F32)
    q_buf = pltpu.VMEM((2 * tq, V_DIM), BF16)
    acc_buf = pltpu.VMEM((2 * tq, 2 * V_DIM), F32)
    resident = pl.Buffered(1)
    vmem = s * V_DIM * 2 + s * 2 * V_DIM * 2 + 2 * (2 * tq * V_DIM * 2 + tq * V_DIM * 2)
    vmem += 2 * tq * (V_DIM * 2 + 3 * LANES * 4 + 2 * V_DIM * 4) + (2 * 4 + 2 * 2 + 3 * 4) * 2 * tq * tk

    def call(kern, scratch, name):
        return pl.pallas_call(
            functools.partial(kern, tq=tq),
            grid=(b, n_heads, nq),
            in_specs=[pl.BlockSpec(memory_space=pltpu.SMEM), qspec, qspec,
                      pl.BlockSpec((s, V_DIM), lambda bi, hi, qi: (bi, hi), pipeline_mode=resident),
                      pl.BlockSpec((s, 2 * V_DIM), lambda bi, hi, qi: (bi, hi), pipeline_mode=resident),
                      pl.BlockSpec((1, V_DIM), lambda bi, hi, qi: (0, 0))],
            out_specs=qspec,
            out_shape=jax.ShapeDtypeStruct((b * s, n_heads * V_DIM), BF16),
            scratch_shapes=scratch,
            compiler_params=_params(3, vmem + 8 * MIB),
            name=name,
        )

    bounded = call(_attn_prompt_bounded_kernel, [q_buf, prob_buf, prob_buf, acc_buf], "attn_prompt_bounded")
    general = call(_attn_prompt_kernel, [q_buf, score_buf, score_buf, prob_buf, prob_buf,
                                         row_buf, row_buf, row_buf, acc_buf], "attn_prompt")
    return lax.cond(score_bound < SCORE_BOUND_FOR_PLAIN_EXP2, bounded, general, lam, qa, qb, kb, vx, sub_gain)


def _attn_sample_kernel(lam_ref, qa_ref, qb_ref, kn_ref, vn_ref, ck_ref, cv_ref, sg_ref, o_ref,
                        *, n_heads, ds):
    for hh in range(n_heads):
        cols = slice(hh * V_DIM, (hh + 1) * V_DIM)
        q2 = jnp.concatenate([qa_ref[:, cols], qb_ref[:, cols]], axis=0)
        s_past = _nt_dot(q2, ck_ref[0, 0, :, cols].astype(BF16))
        s_new = _nt_dot(q2, kn_ref[:, cols])
        m = jnp.maximum(jnp.max(s_past, axis=-1, keepdims=True),
                        jnp.max(s_new, axis=-1, keepdims=True))
        p_past = jnp.exp2(s_past - m)
        p_new = jnp.exp2(s_new - m)
        denom = jnp.sum(p_past, axis=-1, keepdims=True) + jnp.sum(p_new, axis=-1, keepdims=True)
        pv = (_dot(p_past.astype(BF16), cv_ref[0, 0, :, cols].astype(BF16))
              + _dot(p_new.astype(BF16), vn_ref[:, 2 * hh * V_DIM:(2 * hh + 1) * V_DIM]))
        o_ref[:, cols] = _diff_finish(pv / denom, lam_ref[0], sg_ref[...], ds).astype(BF16)


def _attn_sample(lam, qa, qb, kb, vx, cache_k, cache_v, sub_gain, *, layer, n_heads, ds, row0):
    _, db, past, aw = cache_k.shape
    blk0 = row0 // ds
    new = lambda w: pl.BlockSpec((ds, w), lambda d: (blk0 + d, 0))
    cache = pl.BlockSpec((1, 1, past, aw), lambda d: (layer, d, 0, 0))
    vmem = 2 * (2 * past * aw * 4 + 6 * ds * aw * 2) + 8 * 2 * ds * past * 4 + 2 * past * V_DIM * 2
    return pl.pallas_call(
        functools.partial(_attn_sample_kernel, n_heads=n_heads, ds=ds),
        grid=(db,),
        in_specs=[pl.BlockSpec(memory_space=pltpu.SMEM), new(aw), new(aw), new(aw), new(2 * aw),
                  cache, cache, pl.BlockSpec((1, V_DIM), lambda d: (0, 0))],
        out_specs=pl.BlockSpec((ds, aw), lambda d: (d, 0)),
        out_shape=jax.ShapeDtypeStruct((db * ds, aw), BF16),
        compiler_params=_params(1, vmem + 8 * MIB),
        name="attn_sample",
    )(lam, qa, qb, kb, vx, cache_k, cache_v, sub_gain)


def _pool_conv_kernel(u_ref, g_ref, hu_ref, hg_ref, pw_ref, ps_ref, cw_ref, cb_ref, lg_ref, lb_ref, cpw_ref,
                      pool_ref, conv_ref, pu_sc, pg_sc, *, tm, tiles_per_seq, pos0, sub):
    i = pl.program_id(0)
    u = u_ref[...]
    pu_sc[0:POOL_HALO] = hu_ref[0]
    pu_sc[POOL_HALO:POOL_HALO + tm] = u
    pg_sc[0:CONV_HALO] = hg_ref[0]
    pg_sc[CONV_HALO:CONV_HALO + tm] = g_ref[...]

    padded = pu_sc[...]
    sums = []
    acc, width = padded, 1
    for win in POOL_WINDOWS:
        while width < win:
            acc = acc + pltpu.roll(acc, width, axis=0)
            width *= 2
        sums.append(acc[POOL_HALO:POOL_HALO + tm])
    c = u.shape[1]
    group = lax.broadcasted_iota(I32, (tm, c), 1) // (c // len(POOL_WINDOWS))
    pos = pos0 + (i % tiles_per_seq) * tm + lax.broadcasted_iota(I32, (tm, c), 0)
    tot, win_lane = sums[-1], jnp.full((tm, c), POOL_WINDOWS[-1], I32)
    for gi in range(len(POOL_WINDOWS) - 2, -1, -1):
        tot = jnp.where(group == gi, sums[gi], tot)
        win_lane = jnp.where(group == gi, POOL_WINDOWS[gi], win_lane)
    cnt = jnp.minimum(pos + 1, win_lane).astype(F32)
    d = tot / cnt - u
    pool_ref[...] = (_dot(d.astype(BF16), pw_ref[...]) * ps_ref[...]).astype(BF16)

    for r0 in range(0, tm, sub):
        y = jnp.zeros((sub, c), F32)
        for j in range(CONV_K):
            lo = r0 + CONV_HALO - (CONV_K - 1) + j
            y = y + pg_sc[lo:lo + sub, :] * cw_ref[j:j + 1, :]
        y = y + cb_ref[...]
        mu = jnp.mean(y, axis=-1, keepdims=True)
        yc = y - mu
        yn = yc * lax.rsqrt(jnp.mean(yc * yc, axis=-1, keepdims=True) + EPS) * lg_ref[...] + lb_ref[...]
        act = yn * jax.nn.sigmoid(yn)
        conv_ref[r0:r0 + sub, :] = _dot(act.astype(BF16), cpw_ref[...]).astype(BF16)


def _pool_conv(u, g, halo_u, halo_g, pool_wbd, pool_scale, conv_w, conv_b, ln_g, ln_b, conv_pw,
               *, row0, n_rows, tm, tiles_per_seq, pos0):
    c = u.shape[1]
    n_tiles = n_rows // tm
    blk0 = row0 // tm
    sub = _pick_tile(tm, (64, 32))
    row_in = pl.BlockSpec((tm, c), lambda i: (blk0 + i, 0))
    row_out = pl.BlockSpec((tm, c), lambda i: (i, 0))
    full = lambda a: pl.BlockSpec(a.shape, lambda i: (0,) * a.ndim)
    vmem = 2 * (2 * tm * c * 4 + 2 * tm * c * 2) + 14 * (tm + CONV_HALO) * c * 4 + 4 * c * c * 2
    return pl.pallas_call(
        functools.partial(_pool_conv_kernel, tm=tm, tiles_per_seq=tiles_per_seq, pos0=pos0, sub=sub),
        grid=(n_tiles,),
        in_specs=[row_in, row_in,
                  pl.BlockSpec((1, POOL_HALO, c), lambda i: (i, 0, 0)),
                  pl.BlockSpec((1, CONV_HALO, c), lambda i: (i, 0, 0)),
                  full(pool_wbd), full(pool_scale), full(conv_w), full(conv_b), full(ln_g), full(ln_b),
                  full(conv_pw)],
        out_specs=[row_out, row_out],
        out_shape=[jax.ShapeDtypeStruct((n_rows, c), BF16)] * 2,
        scratch_shapes=[pltpu.VMEM((tm + POOL_HALO, c), F32), pltpu.VMEM((tm + CONV_HALO, c), F32)],
        compiler_params=_params(1, vmem + 8 * MIB),
        name="pool_conv",
    )(u, g, halo_u, halo_g, pool_wbd, pool_scale, conv_w, conv_b, ln_g, ln_b, conv_pw)


def _tile_halos(rows, n_seq, seq_len, tm, halo, first):
    c = rows.shape[1]
    tiles = seq_len // tm
    tails = rows.reshape(n_seq, tiles, tm, c)[:, :tiles - 1, tm - halo:, :]
    return jnp.concatenate([first[:, None], tails], axis=1).reshape(n_seq * tiles, halo, c)


def _front_pad(state, halo):
    n, rows, c = state.shape
    return jnp.concatenate([jnp.zeros((n, halo - rows, c), state.dtype), state], axis=1)


def _mix_out(mixer_refs, x_ref, w_ref, xo_ref, n_prompt_tiles):
    def mix(att_ref, pool_ref, conv_ref):
        aw, pw = att_ref.shape[1], pool_ref.shape[1]
        xo_ref[...] = (x_ref[...] + _dot(att_ref[...], w_ref[0:aw, :]) + _dot(pool_ref[...], w_ref[aw:aw + pw, :])
                       + _dot(conv_ref[...], w_ref[aw + pw:, :]))

    is_prompt = pl.program_id(0) < n_prompt_tiles
    pl.when(is_prompt)(lambda: mix(*mixer_refs[:3]))
    pl.when(jnp.logical_not(is_prompt))(lambda: mix(*mixer_refs[3:]))
    return xo_ref[...]


def _out_proj_dense_kernel(*refs, n_prompt_tiles):
    mixer_refs, (x_ref, w_ref, fg_ref, xo_ref, h_ref) = refs[:6], refs[6:]
    x = _mix_out(mixer_refs, x_ref, w_ref, xo_ref, n_prompt_tiles)
    h_ref[...] = _rms_rows(x, fg_ref[...]).astype(BF16)


def _out_proj_router_kernel(*refs, n_prompt_tiles):
    mixer_refs = refs[:6]
    (x_ref, w_ref, fg_ref, rh_ref, rl_ref, tri_ref,
     xo_ref, idx_ref, gate_ref, rank_ref, cnt_ref, carry_sc) = refs[6:]

    @pl.when(pl.program_id(0) == 0)
    def _():
        carry_sc[...] = jnp.zeros(carry_sc.shape, F32)

    x = _mix_out(mixer_refs, x_ref, w_ref, xo_ref, n_prompt_tiles)
    h_hi, h_lo = _split_bf16(_rms_rows(x, fg_ref[...]))
    logits = _nt_dot(rh_ref[...], h_hi) + _nt_dot(rh_ref[...], h_lo) + _nt_dot(rl_ref[...], h_hi)
    n_exp = logits.shape[0]
    eidx = lax.broadcasted_iota(I32, logits.shape, 0)
    m1 = jnp.max(logits, axis=0, keepdims=True)
    i1 = jnp.min(jnp.where(logits == m1, eidx, n_exp), axis=0, keepdims=True)
    rest = jnp.where(eidx == i1, -jnp.inf, logits)
    m2 = jnp.max(rest, axis=0, keepdims=True)
    i2 = jnp.min(jnp.where(rest == m2, eidx, n_exp), axis=0, keepdims=True)
    e2 = jnp.exp(m2 - m1)
    idx_ref[...] = jnp.concatenate([i1, i2], axis=0)
    gate_ref[...] = jnp.concatenate([1.0 / (1.0 + e2), e2 / (1.0 + e2)], axis=0)

    pick1, pick2 = eidx == i1, eidx == i2
    picks = jnp.concatenate([pick1, pick2], axis=0).astype(BF16)
    earlier = _dot(picks, tri_ref[...])
    tot1 = jnp.sum(pick1.astype(F32), axis=1, keepdims=True)
    tot2 = jnp.sum(pick2.astype(F32), axis=1, keepdims=True)
    carry = carry_sc[...]
    base = carry[:, 0:1]
    r1 = jnp.sum(jnp.where(pick1, base + earlier[:n_exp], 0.0), axis=0, keepdims=True)
    r2 = jnp.sum(jnp.where(pick2, base + tot1 + earlier[n_exp:], 0.0), axis=0, keepdims=True)
    rank_ref[...] = jnp.concatenate([r1, r2], axis=0).astype(I32)
    carry = carry + tot1 + tot2
    carry_sc[...] = carry
    cnt_ref[...] = carry.astype(I32)


def _out_proj(mix_prompt, mix_sample, x, w_out, ffn_g, router=None):
    t, d = x.shape
    tp = mix_prompt[0].shape[0]
    tm = _pick_tile(math.gcd(tp, t - tp), (512, 256, 128))
    npt = tp // tm
    row = lambda a: pl.BlockSpec((tm, a.shape[1]), lambda i: (i, 0))
    rows_p = lambda a: pl.BlockSpec((tm, a.shape[1]), lambda i: (jnp.minimum(i, npt - 1), 0))
    rows_s = lambda a: pl.BlockSpec((tm, a.shape[1]), lambda i: (jnp.maximum(i - npt, 0), 0))
    full = lambda a: pl.BlockSpec(a.shape, lambda i: (0,) * a.ndim)
    ins = list(mix_prompt) + list(mix_sample) + [x, w_out, ffn_g]
    in_specs = [rows_p(a) for a in mix_prompt] + [rows_s(a) for a in mix_sample] + [row(x), full(w_out), full(ffn_g)]
    vmem = 2 * (2 * tm * d * 2 + tm * d * 4 * 2 + d * d * 2 + tm * d * 2) + 6 * tm * d * 4
    scratch = []
    if router is None:
        kern = _out_proj_dense_kernel
        out_specs = [row(x), row(x)]
        out_shape = [jax.ShapeDtypeStruct((t, d), F32), jax.ShapeDtypeStruct((t, d), BF16)]
    else:
        kern = _out_proj_router_kernel
        n_exp = router[0].shape[0]
        earlier_token = jnp.arange(tm, dtype=I32)[:, None] < jnp.arange(tm, dtype=I32)[None, :]
        ins += list(router) + [earlier_token.astype(BF16)]
        in_specs += [full(router[0]), full(router[1]), pl.BlockSpec((tm, tm), lambda i: (0, 0))]
        top = pl.BlockSpec((TOP_K, tm), lambda i: (0, i))
        out_specs = [row(x), top, top, top, pl.BlockSpec((n_exp, LANES), lambda i: (0, 0))]
        out_shape = [jax.ShapeDtypeStruct((t, d), F32), jax.ShapeDtypeStruct((TOP_K, t), I32),
                     jax.ShapeDtypeStruct((TOP_K, t), F32), jax.ShapeDtypeStruct((TOP_K, t), I32),
                     jax.ShapeDtypeStruct((n_exp, LANES), I32)]
        scratch = [pltpu.VMEM((n_exp, LANES), F32)]
        vmem += 2 * tm * tm * 2
    return pl.pallas_call(
        functools.partial(kern, n_prompt_tiles=npt),
        grid=(t // tm,), in_specs=in_specs, out_specs=out_specs, out_shape=out_shape,
        scratch_shapes=scratch, compiler_params=_params(1, vmem + 8 * MIB), name="out_proj",
    )(*ins)


def _swiglu_acc(h, acc, wg_ref, wu_ref, wd_ref, lead, tf):
    f = wg_ref.shape[-1]
    for c0 in range(0, f, tf):
        gate = _dot(h, wg_ref[lead + (slice(None), slice(c0, c0 + tf))])
        up = _dot(h, wu_ref[lead + (slice(None), slice(c0, c0 + tf))])
        act = (gate * jax.nn.sigmoid(gate) * up).astype(BF16)
        acc = acc + _dot(act, wd_ref[lead + (slice(c0, c0 + tf), slice(None))])
    return acc


def _ffn_kernel(h_ref, x_ref, wg_ref, wu_ref, wd_ref, o_ref, *, tf):
    o_ref[...] = _swiglu_acc(h_ref[...], x_ref[...], wg_ref, wu_ref, wd_ref, (), tf)


def _ffn(h, x, wg, wu, wd):
    t, d = x.shape
    f = wg.shape[1]
    tm = _pick_tile(t, (512, 256, 128))
    tf = _pick_tile(f, (512, 256, 128))
    row = lambda: pl.BlockSpec((tm, d), lambda i: (i, 0))
    full = lambda a: pl.BlockSpec(a.shape, lambda i: (0,) * a.ndim)
    vmem = 2 * (3 * d * f * 2 + tm * d * (2 + 4 + 4)) + 4 * tm * tf * 4 + 2 * tm * d * 4
    return pl.pallas_call(
        functools.partial(_ffn_kernel, tf=tf),
        grid=(t // tm,),
        in_specs=[row(), row(), full(wg), full(wu), full(wd)],
        out_specs=row(),
        out_shape=jax.ShapeDtypeStruct((t, d), F32),
        compiler_params=_params(1, vmem + 4 * MIB),
        name="ffn",
    )(h, x, wg, wu, wd)


def _moe_kernel(te_ref, tv_ref, src_ref, src_next_ref, dst_ref, fg_ref, wg_ref, wu_ref, wd_ref, x_hbm,
                out_hbm, xbuf, ybuf, gsem, ssem, *, tm, tf, n_tiles):
    i = pl.program_id(0)
    slot = i % 2

    def gather_row(idx_ref, r, s):
        return pltpu.make_async_copy(x_hbm.at[pl.ds(idx_ref[0, 0, r], 1)], xbuf.at[s, pl.ds(r, 1)], gsem.at[s])

    def scatter_row(row, r, s):
        return pltpu.make_async_copy(ybuf.at[s, pl.ds(r, 1)], out_hbm.at[pl.ds(row, 1)], ssem.at[s])

    def for_rows(fn, static=False):
        if static:
            for r in range(tm):
                fn(r)
            return

        def body(r, carry):
            fn(r)
            return carry
        lax.fori_loop(0, tm, body, 0, unroll=8)

    @pl.when(i == 0)
    def _():
        for_rows(lambda r: gather_row(src_ref, r, slot).start())

    @pl.when(i + 1 < n_tiles)
    def _():
        for_rows(lambda r: gather_row(src_next_ref, r, 1 - slot).start(), static=True)

    for_rows(lambda r: gather_row(src_ref, r, slot).wait())

    @pl.when(i >= 2)
    def _():
        for_rows(lambda r: scatter_row(0, r, slot).wait())

    @pl.when(tv_ref[i] > 0)
    def _():
        h = _rms_rows(xbuf[slot], fg_ref[...]).astype(BF16)
        lead = (0,)
        ybuf[slot] = _swiglu_acc(h, jnp.zeros((tm, h.shape[1]), F32), wg_ref, wu_ref, wd_ref, lead, tf)

    @pl.when(tv_ref[i] == 0)
    def _():
        ybuf[slot] = jnp.zeros(ybuf.shape[1:], F32)

    for_rows(lambda r: scatter_row(dst_ref[0, 0, r], r, slot).start(), static=True)

    @pl.when(i == n_tiles - 1)
    def _():
        for_rows(lambda r: scatter_row(0, r, slot).wait())
        if n_tiles > 1:
            for_rows(lambda r: scatter_row(0, r, 1 - slot).wait())


def _moe_route(idx, rank, counts, tm):
    k, t = idx.shape
    a = k * t
    n_exp = counts.shape[0]
    padded = ((counts + tm - 1) // tm) * tm
    ends = jnp.cumsum(padded)
    offs = ends - padded
    pos = rank
    for e in range(n_exp):
        pos = pos + jnp.where(idx == e, offs[e], 0)
    n_pad = a + n_exp * tm
    n_tiles = n_pad // tm
    slot_ids = jnp.arange(n_pad, dtype=I32)
    choice = jnp.full((n_pad,), -1, I32).at[pos.reshape(a)].set(jnp.arange(a, dtype=I32))
    filled = choice >= 0
    src = jnp.where(filled, choice % t, 0)
    spare = a + ((slot_ids // tm) % 2) * tm + slot_ids % tm
    dst = jnp.where(filled, choice, spare)
    starts = jnp.arange(n_tiles, dtype=I32) * tm
    tile_expert = jnp.minimum(jnp.sum((starts[:, None] >= ends[None, :]).astype(I32), axis=1), n_exp - 1)
    tile_valid = (starts < ends[-1]).astype(I32)
    shape3 = (n_tiles, 1, tm)
    return tile_expert, tile_valid, src.reshape(shape3), dst.reshape(shape3)


def _moe(x, idx, gate, rank, counts, ffn_g, wg, wu, wd):
    t, d = x.shape
    n_exp, _, f = wg.shape
    tm = 256
    tf = _pick_tile(f, (512, 256, 128))
    te, tv, src, dst = _moe_route(idx, rank, counts, tm)
    n_tiles = src.shape[0]
    n_out = TOP_K * t + 2 * tm
    smem_rows = lambda nxt: pl.BlockSpec(
        (1, 1, tm), (lambda i, te, tv: (jnp.minimum(i + 1, n_tiles - 1), 0, 0)) if nxt else (lambda i, te, tv: (i, 0, 0)),
        memory_space=pltpu.SMEM)
    wspec = lambda a: pl.BlockSpec((1,) + a.shape[1:], lambda i, te, tv: (te[i], 0, 0))
    vmem = 2 * 3 * d * f * 2 + 4 * tm * d * 4 + 4 * tm * tf * 4 + 3 * tm * d * 4
    out2 = pl.pallas_call(
        functools.partial(_moe_kernel, tm=tm, tf=tf, n_tiles=n_tiles),
        grid_spec=pltpu.PrefetchScalarGridSpec(
            num_scalar_prefetch=2, grid=(n_tiles,),
            in_specs=[smem_rows(False), smem_rows(True), smem_rows(False),
                      pl.BlockSpec(ffn_g.shape, lambda i, te, tv: (0, 0)),
                      wspec(wg), wspec(wu), wspec(wd),
                      pl.BlockSpec(memory_space=pl.ANY)],
            out_specs=pl.BlockSpec(memory_space=pl.ANY),
            scratch_shapes=[pltpu.VMEM((2, tm, d), F32), pltpu.VMEM((2, tm, d), F32),
                            pltpu.SemaphoreType.DMA((2,)), pltpu.SemaphoreType.DMA((2,))]),
        out_shape=jax.ShapeDtypeStruct((n_out, d), F32),
        compiler_params=_params(1, vmem + 4 * MIB),
        name="moe_experts",
    )(te, tv, src, src, dst, ffn_g, wg, wu, wd, x)

    tc = _pick_tile(t, (512, 256, 128))
    blocks = t // tc
    return pl.pallas_call(
        _moe_combine_kernel,
        grid=(blocks,),
        in_specs=[pl.BlockSpec((tc, d), lambda i: (i, 0)), pl.BlockSpec((tc, TOP_K), lambda i: (i, 0))]
        + [pl.BlockSpec((tc, d), functools.partial(lambda i, kk: (kk * blocks + i, 0), kk=kk)) for kk in range(TOP_K)],
        out_specs=pl.BlockSpec((tc, d), lambda i: (i, 0)),
        out_shape=jax.ShapeDtypeStruct((t, d), F32),
        compiler_params=_params(1, 2 * 4 * tc * d * 4 + 4 * MIB),
        name="moe_combine",
    )(x, gate.T, *([out2] * TOP_K))


def _moe_combine_kernel(x_ref, g_ref, *refs):
    *y_refs, o_ref = refs
    acc = x_ref[...]
    g = g_ref[...]
    for kk, y_ref in enumerate(y_refs):
        acc = acc + g[:, kk:kk + 1] * y_ref[...]
    o_ref[...] = acc


def kernel(x_prompt, x_sample, cache_k, cache_v, state_pool, state_conv, attn_norm_g, w_in, q_norm_g, k_norm_g,
           lambda_q1, lambda_k1, lambda_q2, lambda_k2, subln_g, pool_w, pool_scale, conv_w, conv_b, conv_ln_g,
           conv_ln_b, conv_pw, w_out, ffn_norm_g, ffn_w_gate, ffn_w_up, ffn_w_down, router_w, moe_w_gate,
           moe_w_up, moe_w_down):
    bp, sp, d = x_prompt.shape
    db, ds, _ = x_sample.shape
    depth = w_in.shape[0]
    past = cache_k.shape[2]
    n_sub = cache_k.shape[3]
    n_heads = n_sub // 2
    aw = n_heads * V_DIM
    pw = state_pool.shape[-1]
    cw = state_conv.shape[-1]
    pool_rows, conv_rows = state_pool.shape[2], state_conv.shape[2]
    tp, ts = bp * sp, db * ds
    assert pool_rows == max(POOL_WINDOWS) - 1 and conv_rows == CONV_K - 1
    assert w_in.shape[2] == 3 * aw + pw + 2 * cw and ds >= conv_rows and tp % ds == 0

    x = jnp.concatenate([x_prompt.reshape(tp, d), x_sample.reshape(ts, d)], axis=0)
    ck = cache_k.reshape(depth, db, past, aw)
    cv = cache_v.reshape(depth, db, past, aw)

    lane_group = jnp.arange(aw, dtype=I32) // HEAD_DIM
    gsum = (lane_group[:, None] == lane_group[None, :]).astype(BF16)
    n_groups = len(POOL_WINDOWS)
    pg = pw // n_groups
    tm_pool = _pick_tile(sp, (512, 256, 128))

    outs = {name: [] for name in ("kp", "vp", "pp", "cp", "ks", "vs", "ps", "cs")}
    for l in range(depth):
        lam_init = 0.8 - 0.6 * math.exp(-0.3 * l)
        lam = (jnp.exp(jnp.sum(lambda_q1[l] * lambda_k1[l])) - jnp.exp(jnp.sum(lambda_q2[l] * lambda_k2[l]))
               + lam_init).astype(F32).reshape(1)
        qg = (jnp.tile(q_norm_g[l], n_sub) * (HEAD_DIM ** -0.5 * math.log2(math.e))).reshape(1, aw)
        kg = jnp.tile(k_norm_g[l], n_sub).reshape(1, aw)
        sub_gain = (subln_g[l] * (1.0 - lam_init)).reshape(1, V_DIM)

        (qa, qb, kb, vx, u, g), (kp, vp, ks, vs) = _in_proj(
            x, attn_norm_g[l].reshape(1, d), w_in[l].astype(BF16), gsum, qg, kg, tp=tp, aw=aw, pw=pw, cw=cw)

        score_bound = HEAD_DIM * jnp.max(jnp.abs(qg)) * jnp.max(jnp.abs(kg)) * 1.02
        att_p = _attn_prompt(lam, qa, qb, kb, vx, sub_gain, score_bound, n_heads=n_heads, b=bp, s=sp)
        att_s = _attn_sample(lam, qa, qb, kb, vx, ck, cv, sub_gain, layer=l, n_heads=n_heads, ds=ds, row0=tp)

        pool_wbd = jax.scipy.linalg.block_diag(*[pool_w[l, gi] for gi in range(n_groups)]).astype(BF16)
        mix_w = (pool_wbd, pool_scale[l].reshape(1, pw), conv_w[l], conv_b[l].reshape(1, cw),
                 conv_ln_g[l].reshape(1, cw), conv_ln_b[l].reshape(1, cw), conv_pw[l].astype(BF16))
        zeros_p = lambda halo, c: jnp.zeros((bp, halo, c), F32)
        pool_p, conv_p = _pool_conv(
            u, g, _tile_halos(u[:tp], bp, sp, tm_pool, POOL_HALO, zeros_p(POOL_HALO, pw)),
            _tile_halos(g[:tp], bp, sp, tm_pool, CONV_HALO, zeros_p(CONV_HALO, cw)), *mix_w,
            row0=0, n_rows=tp, tm=tm_pool, tiles_per_seq=sp // tm_pool, pos0=0)
        pool_s, conv_s = _pool_conv(
            u, g, _front_pad(state_pool[l], POOL_HALO), _front_pad(state_conv[l], CONV_HALO), *mix_w,
            row0=tp, n_rows=ts, tm=ds, tiles_per_seq=1, pos0=past)
        mixed = ((att_p, pool_p, conv_p), (att_s, pool_s, conv_s))

        w_o = w_out[l].astype(BF16)
        fg = ffn_norm_g[l].reshape(1, d)
        i = l // 2
        if l % 2 == 0:
            x, h = _out_proj(*mixed, x, w_o, fg)
            x = _ffn(h, x, ffn_w_gate[i].astype(BF16), ffn_w_up[i].astype(BF16), ffn_w_down[i].astype(BF16))
        else:
            x, idx, gate, rank, counts = _out_proj(*mixed, x, w_o, fg, router=_split_bf16(router_w[i].T))
            x = _moe(x, idx, gate, rank, counts[:, 0], fg, moe_w_gate[i].astype(BF16), moe_w_up[i].astype(BF16),
                     moe_w_down[i].astype(BF16))

        outs["kp"].append(kp.reshape(bp, sp, n_sub, HEAD_DIM))
        outs["vp"].append(vp.reshape(bp, sp, n_heads, V_DIM))
        outs["pp"].append(u[:tp].reshape(bp, sp, pw)[:, sp - pool_rows:])
        outs["cp"].append(g[:tp].reshape(bp, sp, cw)[:, sp - conv_rows:])
        outs["ks"].append(ks.reshape(db, ds, n_sub, HEAD_DIM))
        outs["vs"].append(vs.reshape(db, ds, n_heads, V_DIM))
        outs["ps"].append(u[tp:].reshape(db, ds, pw)[:, ds - pool_rows:])
        outs["cs"].append(g[tp:].reshape(db, ds, cw)[:, ds - conv_rows:])

    stack = lambda name: jnp.stack(outs[name])
    return (x[:tp].reshape(bp, sp, d), x[tp:].reshape(db, ds, d),
            stack("kp"), stack("vp"), stack("pp"), stack("cp"),
            stack("ks"), stack("vs"), stack("ps"), stack("cs"))
```

```python
import functools
import math

import jax
import jax.numpy as jnp
from jax import lax
from jax.experimental import pallas as pl
from jax.experimental.pallas import tpu as pltpu

F32 = jnp.float32
BF16 = jnp.bfloat16
I32 = jnp.int32

EPS = 1e-6
CHUNK = 64
HEAD_DIM = 64
V_DIM = 2 * HEAD_DIM
POOL_WINDOWS = (2, 4, 8, 16)
POOL_HALO = 16
CONV_K = 31
CONV_HALO = 32
TOP_K = 2
NEG_BIG = -1e30

LANES = 128
MIB = 1024 * 1024
VMEM_CAP_BYTES = 60 * MIB


def _pick_tile(n, candidates):
    for c in candidates:
        if n % c == 0:
            return c
    raise ValueError(f"no tile in {candidates} divides {n}")


def _params(n_grid_dims, vmem_bytes):
    return pltpu.CompilerParams(dimension_semantics=("arbitrary",) * n_grid_dims,
                                vmem_limit_bytes=int(min(vmem_bytes, VMEM_CAP_BYTES)))


def _nt_dot(a, b):
    return lax.dot_general(a, b, (((1,), (1,)), ((), ())), preferred_element_type=F32)


def _dot(a, b):
    return jnp.dot(a, b, preferred_element_type=F32)


def _split_bf16(x):
    hi = x.astype(BF16)
    lo = (x - hi.astype(F32)).astype(BF16)
    return hi, lo


def _rms_rows(x, gain):
    ms = jnp.mean(x * x, axis=-1, keepdims=True)
    return x * lax.rsqrt(ms + EPS) * gain


def _in_proj_kernel(x_ref, ng_ref, w_ref, gsum_ref, qg_ref, kg_ref,
                    qa_ref, qb_ref, kb_ref, vx_ref, u_ref, g_ref, kp_ref, vp_ref, ks_ref, vs_ref,
                    *, aw, pw, cw, n_prompt_tiles):
    h = _rms_rows(x_ref[...], ng_ref[...]).astype(BF16)

    def proj(lo, width):
        return _dot(h, w_ref[:, lo:lo + width])

    def head_norm(t, gain):
        ss = _dot((t * t).astype(BF16), gsum_ref[...])
        return t * lax.rsqrt(ss * (1.0 / HEAD_DIM) + EPS) * gain

    q = head_norm(proj(0, aw), qg_ref[...])
    lane = lax.broadcasted_iota(I32, q.shape, 1)
    first_map = (lane // HEAD_DIM) % 2 == 0
    qa_ref[...] = jnp.where(first_map, q, 0.0).astype(BF16)
    qb_ref[...] = jnp.where(first_map, 0.0, q).astype(BF16)

    k = head_norm(proj(aw, aw), kg_ref[...])
    kb_ref[...] = k.astype(BF16)
    v = proj(2 * aw, aw)
    vb = v.astype(BF16)
    ones = jnp.ones((vb.shape[0], V_DIM), BF16)
    for hh in range(aw // V_DIM):
        vx_ref[:, 2 * hh * V_DIM:(2 * hh + 1) * V_DIM] = vb[:, hh * V_DIM:(hh + 1) * V_DIM]
        vx_ref[:, (2 * hh + 1) * V_DIM:(2 * hh + 2) * V_DIM] = ones

    is_prompt = pl.program_id(0) < n_prompt_tiles

    @pl.when(is_prompt)
    def _():
        kp_ref[...] = k
        vp_ref[...] = v

    @pl.when(jnp.logical_not(is_prompt))
    def _():
        ks_ref[...] = k
        vs_ref[...] = v

    u_ref[...] = proj(3 * aw, pw)
    a = proj(3 * aw + pw, cw)
    gate = proj(3 * aw + pw + cw, cw)
    g_ref[...] = a * jax.nn.sigmoid(gate)


def _in_proj(x, norm_g, w_in, gsum, qg, kg, *, tp, aw, pw, cw):
    t, d = x.shape
    ts = t - tp
    tm = _pick_tile(math.gcd(tp, ts), (512, 256, 128))
    npt = tp // tm
    in_w = w_in.shape[1]
    row = lambda w: pl.BlockSpec((tm, w), lambda i: (i, 0))
    full = lambda a: pl.BlockSpec(a.shape, lambda i: (0,) * a.ndim)
    out_widths = (aw, aw, aw, 2 * aw, pw, cw)
    out_dtypes = (BF16, BF16, BF16, BF16, F32, F32)
    rows_p = pl.BlockSpec((tm, aw), lambda i: (jnp.minimum(i, npt - 1), 0))
    rows_s = pl.BlockSpec((tm, aw), lambda i: (jnp.maximum(i - npt, 0), 0))
    vmem = 2 * (tm * d * 4 + d * in_w * 2 + 4 * tm * aw * 4
                + sum(tm * w * jnp.dtype(dt).itemsize for w, dt in zip(out_widths, out_dtypes)))
    vmem += 8 * tm * aw * 4 + 2 * aw * aw * 2
    outs = pl.pallas_call(
        functools.partial(_in_proj_kernel, aw=aw, pw=pw, cw=cw, n_prompt_tiles=npt),
        grid=(t // tm,),
        in_specs=[row(d), full(norm_g), full(w_in), full(gsum), full(qg), full(kg)],
        out_specs=[row(w) for w in out_widths] + [rows_p, rows_p, rows_s, rows_s],
        out_shape=[jax.ShapeDtypeStruct((t, w), dt) for w, dt in zip(out_widths, out_dtypes)]
        + [jax.ShapeDtypeStruct((rows, aw), F32) for rows in (tp, tp, ts, ts)],
        compiler_params=_params(1, vmem + 8 * MIB),
        name="in_proj",
    )(x, norm_g, w_in, gsum, qg, kg)
    return outs[:len(out_widths)], outs[len(out_widths):]


def _diff_finish(o, lam, sub_gain, rows):
    a = o[:rows] - lam * o[rows:]
    return _rms_rows(a, sub_gain)


def _attn_prompt_kernel(lam_ref, qa_ref, qb_ref, k_ref, vx_ref, sg_ref, o_ref,
                        q_sc, s0_sc, s1_sc, p0_sc, p1_sc, a0_sc, a1_sc, m_sc, acc_sc, *, tq):
    i = pl.program_id(2)
    tk = tq
    q_sc[0:tq] = qa_ref[...]
    q_sc[tq:2 * tq] = qb_ref[...]
    m_sc[...] = jnp.full(m_sc.shape, NEG_BIG, F32)
    acc_sc[...] = jnp.zeros(acc_sc.shape, F32)
    p1_sc[...] = jnp.zeros(p1_sc.shape, BF16)
    a1_sc[...] = jnp.ones(a1_sc.shape, F32)
    even = (s0_sc, p0_sc, a0_sc)
    odd = (s1_sc, p1_sc, a1_sc)

    def rows(j):
        return pl.ds(pl.multiple_of(j * tk, tk), tk)

    def scores(j):
        return _nt_dot(q_sc[...], k_ref[rows(j), :])

    def softmax(buf, masked):
        s_ref, p_ref, a_ref = buf
        s = s_ref[...]
        if masked:
            q_chunk = (lax.broadcasted_iota(I32, s.shape, 0) % tq) // CHUNK
            k_chunk = lax.broadcasted_iota(I32, s.shape, 1) // CHUNK
            s = jnp.where(k_chunk <= q_chunk, s, NEG_BIG)
        m_old = m_sc[...]
        m_new = jnp.maximum(m_old, jnp.max(s, axis=-1, keepdims=True))
        a_ref[...] = jnp.exp2(m_old - m_new)
        p_ref[...] = jnp.exp2(s - jnp.tile(m_new, (1, tk // LANES))).astype(BF16)
        m_sc[...] = m_new

    def values(buf, j):
        _, p_ref, a_ref = buf
        acc_sc[...] = (acc_sc[...] * jnp.tile(a_ref[...], (1, 2 * V_DIM // LANES))
                       + _dot(p_ref[...], vx_ref[rows(j), :]))

    def step(cur, nxt, j):
        softmax(cur, masked=False)
        nxt[0][...] = scores(j + 1)
        values(nxt, jnp.maximum(j - 1, 0))

    def by_parity(n, fn):
        pl.when(n % 2 == 0)(lambda: fn(even, odd))
        pl.when(n % 2 == 1)(lambda: fn(odd, even))

    s0_sc[...] = scores(0)

    def body(j, carry):
        by_parity(j, lambda cur, nxt: step(cur, nxt, j))
        return carry

    lax.fori_loop(0, i, body, 0)

    def boundary_tile(cur, prv):
        softmax(cur, masked=True)
        values(prv, jnp.maximum(i - 1, 0))

    by_parity(i, boundary_tile)
    by_parity(i, lambda cur, prv: values(cur, i))

    acc = acc_sc[...]
    o = acc[:, :V_DIM] / acc[:, V_DIM:]
    o_ref[...] = _diff_finish(o, lam_ref[0], sg_ref[...], tq).astype(BF16)


def _attn_prompt_bounded_kernel(lam_ref, qa_ref, qb_ref, k_ref, vx_ref, sg_ref, o_ref,
                                q_sc, p0_sc, p1_sc, acc_sc, *, tq, tk):
    i = pl.program_id(2)
    q_sc[0:tq] = qa_ref[...]
    q_sc[tq:2 * tq] = qb_ref[...]
    acc_sc[...] = jnp.zeros(acc_sc.shape, F32)

    def rows(j):
        return pl.ds(pl.multiple_of(j * tk, tk), tk)

    def probs(j, boundary_half=None):
        s = _nt_dot(q_sc[...], k_ref[rows(j), :])
        if boundary_half is not None:
            q_chunk = (lax.broadcasted_iota(I32, s.shape, 0) % tq) // CHUNK
            k_chunk = (boundary_half * tk + lax.broadcasted_iota(I32, s.shape, 1)) // CHUNK
            s = jnp.where(k_chunk <= q_chunk, s, NEG_BIG)
        return jnp.exp2(s).astype(BF16)

    def values(p_ref, j):
        acc_sc[...] += _dot(p_ref[...], vx_ref[rows(j), :])

    def step(cur, nxt, j):
        nxt[...] = probs(j + 1)
        values(cur, j)

    def body(j, carry):
        pl.when(j % 2 == 0)(lambda: step(p0_sc, p1_sc, j))
        pl.when(j % 2 == 1)(lambda: step(p1_sc, p0_sc, j))
        return carry

    @pl.when(i > 0)
    def _():
        p0_sc[...] = probs(0)

    lax.fori_loop(0, jnp.maximum(2 * i - 1, 0), body, 0)

    @pl.when(i > 0)
    def _():
        p0_sc[...] = probs(2 * i, boundary_half=0)
        values(p1_sc, 2 * i - 1)

    @pl.when(i == 0)
    def _():
        p0_sc[...] = probs(0, boundary_half=0)

    own_region = pl.when(i >= 0)

    @own_region
    def _():
        p1_sc[...] = probs(2 * i + 1, boundary_half=1)
        values(p0_sc, 2 * i)

    @own_region
    def _():
        values(p1_sc, 2 * i + 1)

    acc = acc_sc[...]
    o = acc[:, :V_DIM] / acc[:, V_DIM:]
    o_ref[...] = _diff_finish(o, lam_ref[0], sg_ref[...], tq).astype(BF16)


SCORE_BOUND_FOR_PLAIN_EXP2 = 64.0


def _attn_prompt(lam, qa, qb, kb, vx, sub_gain, score_bound, *, n_heads, b, s):
    resident = pl.Buffered(1)

    def call(kern, tq, tk, scratch, name):
        nq = s // tq
        qspec = pl.BlockSpec((tq, V_DIM), lambda bi, hi, qi: (bi * nq + qi, hi))
        buf = {"score": pltpu.VMEM((2 * tq, tk), F32), "prob": pltpu.VMEM((2 * tq, tk), BF16),
               "row": pltpu.VMEM((2 * tq, LANES), F32), "q": pltpu.VMEM((2 * tq, V_DIM), BF16),
               "acc": pltpu.VMEM((2 * tq, 2 * V_DIM), F32)}
        vmem = s * V_DIM * 2 + s * 2 * V_DIM * 2 + 2 * (2 * tq * V_DIM * 2 + tq * V_DIM * 2)
        vmem += 2 * tq * (V_DIM * 2 + 3 * LANES * 4 + 2 * V_DIM * 4) + (2 * 4 + 2 * 2 + 3 * 4) * 2 * tq * tk
        return pl.pallas_call(
            kern,
            grid=(b, n_heads, nq),
            in_specs=[pl.BlockSpec(memory_space=pltpu.SMEM), qspec, qspec,
                      pl.BlockSpec((s, V_DIM), lambda bi, hi, qi: (bi, hi), pipeline_mode=resident),
                      pl.BlockSpec((s, 2 * V_DIM), lambda bi, hi, qi: (bi, hi), pipeline_mode=resident),
                      pl.BlockSpec((1, V_DIM), lambda bi, hi, qi: (0, 0))],
            out_specs=qspec,
            out_shape=jax.ShapeDtypeStruct((b * s, n_heads * V_DIM), BF16),
            scratch_shapes=[buf[n] for n in scratch],
            compiler_params=_params(3, vmem + 8 * MIB),
            name=name,
        )

    tq_b = _pick_tile(s, (1024, 512, 256))
    bounded = call(functools.partial(_attn_prompt_bounded_kernel, tq=tq_b, tk=tq_b // 2), tq_b, tq_b // 2,
                   ("q", "prob", "prob", "acc"), "attn_prompt_bounded")
    tq_g = _pick_tile(s, (512, 256, 128))
    general = call(functools.partial(_attn_prompt_kernel, tq=tq_g), tq_g, tq_g,
                   ("q", "score", "score", "prob", "prob", "row", "row", "row", "acc"), "attn_prompt")
    return lax.cond(score_bound < SCORE_BOUND_FOR_PLAIN_EXP2, bounded, general, lam, qa, qb, kb, vx, sub_gain)


def _attn_sample_kernel(lam_ref, qa_ref, qb_ref, kn_ref, vn_ref, ck_ref, cv_ref, sg_ref, o_ref,
                        *, n_heads, ds):
    for hh in range(n_heads):
        cols = slice(hh * V_DIM, (hh + 1) * V_DIM)
        q2 = jnp.concatenate([qa_ref[:, cols], qb_ref[:, cols]], axis=0)
        s_past = _nt_dot(q2, ck_ref[0, 0, :, cols].astype(BF16))
        s_new = _nt_dot(q2, kn_ref[:, cols])
        m = jnp.maximum(jnp.max(s_past, axis=-1, keepdims=True),
                        jnp.max(s_new, axis=-1, keepdims=True))
        p_past = jnp.exp2(s_past - m)
        p_new = jnp.exp2(s_new - m)
        denom = jnp.sum(p_past, axis=-1, keepdims=True) + jnp.sum(p_new, axis=-1, keepdims=True)
        pv = (_dot(p_past.astype(BF16), cv_ref[0, 0, :, cols].astype(BF16))
              + _dot(p_new.astype(BF16), vn_ref[:, 2 * hh * V_DIM:(2 * hh + 1) * V_DIM]))
        o_ref[:, cols] = _diff_finish(pv / denom, lam_ref[0], sg_ref[...], ds).astype(BF16)


def _attn_sample(lam, qa, qb, kb, vx, cache_k, cache_v, sub_gain, *, layer, n_heads, ds, row0):
    _, db, past, aw = cache_k.shape
    blk0 = row0 // ds
    new = lambda w: pl.BlockSpec((ds, w), lambda d: (blk0 + d, 0))
    cache = pl.BlockSpec((1, 1, past, aw), lambda d: (layer, d, 0, 0))
    vmem = 2 * (2 * past * aw * 4 + 6 * ds * aw * 2) + 8 * 2 * ds * past * 4 + 2 * past * V_DIM * 2
    return pl.pallas_call(
        functools.partial(_attn_sample_kernel, n_heads=n_heads, ds=ds),
        grid=(db,),
        in_specs=[pl.BlockSpec(memory_space=pltpu.SMEM), new(aw), new(aw), new(aw), new(2 * aw),
                  cache, cache, pl.BlockSpec((1, V_DIM), lambda d: (0, 0))],
        out_specs=pl.BlockSpec((ds, aw), lambda d: (d, 0)),
        out_shape=jax.ShapeDtypeStruct((db * ds, aw), BF16),
        compiler_params=_params(1, vmem + 8 * MIB),
        name="attn_sample",
    )(lam, qa, qb, kb, vx, cache_k, cache_v, sub_gain)


def _pool_conv_kernel(u_ref, g_ref, hu_ref, hg_ref, pw_ref, ps_ref, cw_ref, cb_ref, lg_ref, lb_ref, cpw_ref,
                      pool_ref, conv_ref, pu_sc, pg_sc, *, tm, tiles_per_seq, pos0, sub):
    i = pl.program_id(0)
    u = u_ref[...]
    pu_sc[0:POOL_HALO] = hu_ref[0]
    pu_sc[POOL_HALO:POOL_HALO + tm] = u
    pg_sc[0:CONV_HALO] = hg_ref[0]
    pg_sc[CONV_HALO:CONV_HALO + tm] = g_ref[...]

    padded = pu_sc[...]
    sums = []
    acc, width = padded, 1
    for win in POOL_WINDOWS:
        while width < win:
            acc = acc + pltpu.roll(acc, width, axis=0)
            width *= 2
        sums.append(acc[POOL_HALO:POOL_HALO + tm])
    c = u.shape[1]
    group = lax.broadcasted_iota(I32, (tm, c), 1) // (c // len(POOL_WINDOWS))
    pos = pos0 + (i % tiles_per_seq) * tm + lax.broadcasted_iota(I32, (tm, c), 0)
    tot, win_lane = sums[-1], jnp.full((tm, c), POOL_WINDOWS[-1], I32)
    for gi in range(len(POOL_WINDOWS) - 2, -1, -1):
        tot = jnp.where(group == gi, sums[gi], tot)
        win_lane = jnp.where(group == gi, POOL_WINDOWS[gi], win_lane)
    cnt = jnp.minimum(pos + 1, win_lane).astype(F32)
    d = tot / cnt - u
    pool_ref[...] = (_dot(d.astype(BF16), pw_ref[...]) * ps_ref[...]).astype(BF16)

    for r0 in range(0, tm, sub):
        y = jnp.zeros((sub, c), F32)
        for j in range(CONV_K):
            lo = r0 + CONV_HALO - (CONV_K - 1) + j
            y = y + pg_sc[lo:lo + sub, :] * cw_ref[j:j + 1, :]
        y = y + cb_ref[...]
        mu = jnp.mean(y, axis=-1, keepdims=True)
        yc = y - mu
        yn = yc * lax.rsqrt(jnp.mean(yc * yc, axis=-1, keepdims=True) + EPS) * lg_ref[...] + lb_ref[...]
        act = yn * jax.nn.sigmoid(yn)
        conv_ref[r0:r0 + sub, :] = _dot(act.astype(BF16), cpw_ref[...]).astype(BF16)


def _pool_conv(u, g, halo_u, halo_g, pool_wbd, pool_scale, conv_w, conv_b, ln_g, ln_b, conv_pw,
               *, row0, n_rows, tm, tiles_per_seq, pos0):
    c = u.shape[1]
    n_tiles = n_rows // tm
    blk0 = row0 // tm
    sub = _pick_tile(tm, (64, 32))
    row_in = pl.BlockSpec((tm, c), lambda i: (blk0 + i, 0))
    row_out = pl.BlockSpec((tm, c), lambda i: (i, 0))
    full = lambda a: pl.BlockSpec(a.shape, lambda i: (0,) * a.ndim)
    vmem = 2 * (2 * tm * c * 4 + 2 * tm * c * 2) + 14 * (tm + CONV_HALO) * c * 4 + 4 * c * c * 2
    return pl.pallas_call(
        functools.partial(_pool_conv_kernel, tm=tm, tiles_per_seq=tiles_per_seq, pos0=pos0, sub=sub),
        grid=(n_tiles,),
        in_specs=[row_in, row_in,
                  pl.BlockSpec((1, POOL_HALO, c), lambda i: (i, 0, 0)),
                  pl.BlockSpec((1, CONV_HALO, c), lambda i: (i, 0, 0)),
                  full(pool_wbd), full(pool_scale), full(conv_w), full(conv_b), full(ln_g), full(ln_b),
                  full(conv_pw)],
        out_specs=[row_out, row_out],
        out_shape=[jax.ShapeDtypeStruct((n_rows, c), BF16)] * 2,
        scratch_shapes=[pltpu.VMEM((tm + POOL_HALO, c), F32), pltpu.VMEM((tm + CONV_HALO, c), F32)],
        compiler_params=_params(1, vmem + 8 * MIB),
        name="pool_conv",
    )(u, g, halo_u, halo_g, pool_wbd, pool_scale, conv_w, conv_b, ln_g, ln_b, conv_pw)


def _tile_halos(rows, n_seq, seq_len, tm, halo, first):
    c = rows.shape[1]
    tiles = seq_len // tm
    tails = rows.reshape(n_seq, tiles, tm, c)[:, :tiles - 1, tm - halo:, :]
    return jnp.concatenate([first[:, None], tails], axis=1).reshape(n_seq * tiles, halo, c)


def _front_pad(state, halo):
    n, rows, c = state.shape
    return jnp.concatenate([jnp.zeros((n, halo - rows, c), state.dtype), state], axis=1)


def _mix_out(mixer_refs, x_ref, w_ref, xo_ref, n_prompt_tiles):
    def mix(att_ref, pool_ref, conv_ref):
        aw, pw = att_ref.shape[1], pool_ref.shape[1]
        xo_ref[...] = (x_ref[...] + _dot(att_ref[...], w_ref[0:aw, :]) + _dot(pool_ref[...], w_ref[aw:aw + pw, :])
                       + _dot(conv_ref[...], w_ref[aw + pw:, :]))

    is_prompt = pl.program_id(0) < n_prompt_tiles
    pl.when(is_prompt)(lambda: mix(*mixer_refs[:3]))
    pl.when(jnp.logical_not(is_prompt))(lambda: mix(*mixer_refs[3:]))
    return xo_ref[...]


def _out_proj_dense_kernel(*refs, n_prompt_tiles):
    mixer_refs, (x_ref, w_ref, fg_ref, xo_ref, h_ref) = refs[:6], refs[6:]
    x = _mix_out(mixer_refs, x_ref, w_ref, xo_ref, n_prompt_tiles)
    h_ref[...] = _rms_rows(x, fg_ref[...]).astype(BF16)


def _out_proj_router_kernel(*refs, n_prompt_tiles):
    mixer_refs = refs[:6]
    (x_ref, w_ref, fg_ref, rh_ref, rl_ref, tri_ref,
     xo_ref, idx_ref, gate_ref, rank_ref, cnt_ref, carry_sc) = refs[6:]

    @pl.when(pl.program_id(0) == 0)
    def _():
        carry_sc[...] = jnp.zeros(carry_sc.shape, F32)

    x = _mix_out(mixer_refs, x_ref, w_ref, xo_ref, n_prompt_tiles)
    h_hi, h_lo = _split_bf16(_rms_rows(x, fg_ref[...]))
    logits = _nt_dot(rh_ref[...], h_hi) + _nt_dot(rh_ref[...], h_lo) + _nt_dot(rl_ref[...], h_hi)
    n_exp = logits.shape[0]
    eidx = lax.broadcasted_iota(I32, logits.shape, 0)
    m1 = jnp.max(logits, axis=0, keepdims=True)
    i1 = jnp.min(jnp.where(logits == m1, eidx, n_exp), axis=0, keepdims=True)
    rest = jnp.where(eidx == i1, -jnp.inf, logits)
    m2 = jnp.max(rest, axis=0, keepdims=True)
    i2 = jnp.min(jnp.where(rest == m2, eidx, n_exp), axis=0, keepdims=True)
    e2 = jnp.exp(m2 - m1)
    idx_ref[...] = jnp.concatenate([i1, i2], axis=0)
    gate_ref[...] = jnp.concatenate([1.0 / (1.0 + e2), e2 / (1.0 + e2)], axis=0)

    pick1, pick2 = eidx == i1, eidx == i2
    picks = jnp.concatenate([pick1, pick2], axis=0).astype(BF16)
    earlier = _dot(picks, tri_ref[...])
    tot1 = jnp.sum(pick1.astype(F32), axis=1, keepdims=True)
    tot2 = jnp.sum(pick2.astype(F32), axis=1, keepdims=True)
    carry = carry_sc[...]
    base = carry[:, 0:1]
    r1 = jnp.sum(jnp.where(pick1, base + earlier[:n_exp], 0.0), axis=0, keepdims=True)
    r2 = jnp.sum(jnp.where(pick2, base + tot1 + earlier[n_exp:], 0.0), axis=0, keepdims=True)
    rank_ref[...] = jnp.concatenate([r1, r2], axis=0).astype(I32)
    carry = carry + tot1 + tot2
    carry_sc[...] = carry
    cnt_ref[...] = carry.astype(I32)


def _out_proj(mix_prompt, mix_sample, x, w_out, ffn_g, router=None):
    t, d = x.shape
    tp = mix_prompt[0].shape[0]
    tm = _pick_tile(math.gcd(tp, t - tp), (512, 256, 128))
    npt = tp // tm
    row = lambda a: pl.BlockSpec((tm, a.shape[1]), lambda i: (i, 0))
    rows_p = lambda a: pl.BlockSpec((tm, a.shape[1]), lambda i: (jnp.minimum(i, npt - 1), 0))
    rows_s = lambda a: pl.BlockSpec((tm, a.shape[1]), lambda i: (jnp.maximum(i - npt, 0), 0))
    full = lambda a: pl.BlockSpec(a.shape, lambda i: (0,) * a.ndim)
    ins = list(mix_prompt) + list(mix_sample) + [x, w_out, ffn_g]
    in_specs = [rows_p(a) for a in mix_prompt] + [rows_s(a) for a in mix_sample] + [row(x), full(w_out), full(ffn_g)]
    vmem = 2 * (2 * tm * d * 2 + tm * d * 4 * 2 + d * d * 2 + tm * d * 2) + 6 * tm * d * 4
    scratch = []
    if router is None:
        kern = _out_proj_dense_kernel
        out_specs = [row(x), row(x)]
        out_shape = [jax.ShapeDtypeStruct((t, d), F32), jax.ShapeDtypeStruct((t, d), BF16)]
    else:
        kern = _out_proj_router_kernel
        n_exp = router[0].shape[0]
        earlier_token = jnp.arange(tm, dtype=I32)[:, None] < jnp.arange(tm, dtype=I32)[None, :]
        ins += list(router) + [earlier_token.astype(BF16)]
        in_specs += [full(router[0]), full(router[1]), pl.BlockSpec((tm, tm), lambda i: (0, 0))]
        top = pl.BlockSpec((TOP_K, tm), lambda i: (0, i))
        out_specs = [row(x), top, top, top, pl.BlockSpec((n_exp, LANES), lambda i: (0, 0))]
        out_shape = [jax.ShapeDtypeStruct((t, d), F32), jax.ShapeDtypeStruct((TOP_K, t), I32),
                     jax.ShapeDtypeStruct((TOP_K, t), F32), jax.ShapeDtypeStruct((TOP_K, t), I32),
                     jax.ShapeDtypeStruct((n_exp, LANES), I32)]
        scratch = [pltpu.VMEM((n_exp, LANES), F32)]
        vmem += 2 * tm * tm * 2
    return pl.pallas_call(
        functools.partial(kern, n_prompt_tiles=npt),
        grid=(t // tm,), in_specs=in_specs, out_specs=out_specs, out_shape=out_shape,
        scratch_shapes=scratch, compiler_params=_params(1, vmem + 8 * MIB), name="out_proj",
    )(*ins)


def _swiglu_acc(h, acc, wg_ref, wu_ref, wd_ref, lead, tf):
    f = wg_ref.shape[-1]
    for c0 in range(0, f, tf):
        gate = _dot(h, wg_ref[lead + (slice(None), slice(c0, c0 + tf))])
        up = _dot(h, wu_ref[lead + (slice(None), slice(c0, c0 + tf))])
        act = (gate * jax.nn.sigmoid(gate) * up).astype(BF16)
        acc = acc + _dot(act, wd_ref[lead + (slice(c0, c0 + tf), slice(None))])
    return acc


def _ffn_kernel(h_ref, x_ref, wg_ref, wu_ref, wd_ref, o_ref, *, tf):
    o_ref[...] = _swiglu_acc(h_ref[...], x_ref[...], wg_ref, wu_ref, wd_ref, (), tf)


def _ffn(h, x, wg, wu, wd):
    t, d = x.shape
    f = wg.shape[1]
    tm = _pick_tile(t, (512, 256, 128))
    tf = _pick_tile(f, (512, 256, 128))
    row = lambda: pl.BlockSpec((tm, d), lambda i: (i, 0))
    full = lambda a: pl.BlockSpec(a.shape, lambda i: (0,) * a.ndim)
    vmem = 2 * (3 * d * f * 2 + tm * d * (2 + 4 + 4)) + 4 * tm * tf * 4 + 2 * tm * d * 4
    return pl.pallas_call(
        functools.partial(_ffn_kernel, tf=tf),
        grid=(t // tm,),
        in_specs=[row(), row(), full(wg), full(wu), full(wd)],
        out_specs=row(),
        out_shape=jax.ShapeDtypeStruct((t, d), F32),
        compiler_params=_params(1, vmem + 4 * MIB),
        name="ffn",
    )(h, x, wg, wu, wd)


def _moe_kernel(te_ref, tv_ref, src_ref, src_next_ref, dst_ref, fg_ref, wg_ref, wu_ref, wd_ref, x_hbm,
                out_hbm, xbuf, ybuf, gsem, ssem, *, tm, tf, n_tiles):
    i = pl.program_id(0)
    slot = i % 2

    def gather_row(idx_ref, r, s):
        return pltpu.make_async_copy(x_hbm.at[pl.ds(idx_ref[0, 0, r], 1)], xbuf.at[s, pl.ds(r, 1)], gsem.at[s])

    def scatter_row(row, r, s):
        return pltpu.make_async_copy(ybuf.at[s, pl.ds(r, 1)], out_hbm.at[pl.ds(row, 1)], ssem.at[s])

    def for_rows(fn, static=False):
        if static:
            for r in range(tm):
                fn(r)
            return

        def body(r, carry):
            fn(r)
            return carry
        lax.fori_loop(0, tm, body, 0, unroll=8)

    @pl.when(i == 0)
    def _():
        for_rows(lambda r: gather_row(src_ref, r, slot).start())

    @pl.when(i + 1 < n_tiles)
    def _():
        for_rows(lambda r: gather_row(src_next_ref, r, 1 - slot).start(), static=True)

    for_rows(lambda r: gather_row(src_ref, r, slot).wait())

    @pl.when(i >= 2)
    def _():
        for_rows(lambda r: scatter_row(0, r, slot).wait())

    @pl.when(tv_ref[i] > 0)
    def _():
        h = _rms_rows(xbuf[slot], fg_ref[...]).astype(BF16)
        lead = (0,)
        ybuf[slot] = _swiglu_acc(h, jnp.zeros((tm, h.shape[1]), F32), wg_ref, wu_ref, wd_ref, lead, tf)

    @pl.when(tv_ref[i] == 0)
    def _():
        ybuf[slot] = jnp.zeros(ybuf.shape[1:], F32)

    for_rows(lambda r: scatter_row(dst_ref[0, 0, r], r, slot).start(), static=True)

    @pl.when(i == n_tiles - 1)
    def _():
        for_rows(lambda r: scatter_row(0, r, slot).wait())
        if n_tiles > 1:
            for_rows(lambda r: scatter_row(0, r, 1 - slot).wait())


def _moe_route(idx, rank, counts, tm):
    k, t = idx.shape
    a = k * t
    n_exp = counts.shape[0]
    padded = ((counts + tm - 1) // tm) * tm
    ends = jnp.cumsum(padded)
    offs = ends - padded
    pos = rank
    for e in range(n_exp):
        pos = pos + jnp.where(idx == e, offs[e], 0)
    n_pad = a + n_exp * tm
    n_tiles = n_pad // tm
    slot_ids = jnp.arange(n_pad, dtype=I32)
    choice = jnp.full((n_pad,), -1, I32).at[pos.reshape(a)].set(jnp.arange(a, dtype=I32))
    filled = choice >= 0
    src = jnp.where(filled, choice % t, 0)
    spare = a + ((slot_ids // tm) % 2) * tm + slot_ids % tm
    dst = jnp.where(filled, choice, spare)
    starts = jnp.arange(n_tiles, dtype=I32) * tm
    tile_expert = jnp.minimum(jnp.sum((starts[:, None] >= ends[None, :]).astype(I32), axis=1), n_exp - 1)
    tile_valid = (starts < ends[-1]).astype(I32)
    shape3 = (n_tiles, 1, tm)
    return tile_expert, tile_valid, src.reshape(shape3), dst.reshape(shape3)


def _moe(x, idx, gate, rank, counts, ffn_g, wg, wu, wd):
    t, d = x.shape
    n_exp, _, f = wg.shape
    tm = 256
    tf = _pick_tile(f, (512, 256, 128))
    te, tv, src, dst = _moe_route(idx, rank, counts, tm)
    n_tiles = src.shape[0]
    n_out = TOP_K * t + 2 * tm
    smem_rows = lambda nxt: pl.BlockSpec(
        (1, 1, tm), (lambda i, te, tv: (jnp.minimum(i + 1, n_tiles - 1), 0, 0)) if nxt else (lambda i, te, tv: (i, 0, 0)),
        memory_space=pltpu.SMEM)
    wspec = lambda a: pl.BlockSpec((1,) + a.shape[1:], lambda i, te, tv: (te[i], 0, 0))
    vmem = 2 * 3 * d * f * 2 + 4 * tm * d * 4 + 4 * tm * tf * 4 + 3 * tm * d * 4
    out2 = pl.pallas_call(
        functools.partial(_moe_kernel, tm=tm, tf=tf, n_tiles=n_tiles),
        grid_spec=pltpu.PrefetchScalarGridSpec(
            num_scalar_prefetch=2, grid=(n_tiles,),
            in_specs=[smem_rows(False), smem_rows(True), smem_rows(False),
                      pl.BlockSpec(ffn_g.shape, lambda i, te, tv: (0, 0)),
                      wspec(wg), wspec(wu), wspec(wd),
                      pl.BlockSpec(memory_space=pl.ANY)],
            out_specs=pl.BlockSpec(memory_space=pl.ANY),
            scratch_shapes=[pltpu.VMEM((2, tm, d), F32), pltpu.VMEM((2, tm, d), F32),
                            pltpu.SemaphoreType.DMA((2,)), pltpu.SemaphoreType.DMA((2,))]),
        out_shape=jax.ShapeDtypeStruct((n_out, d), F32),
        compiler_params=_params(1, vmem + 4 * MIB),
        name="moe_experts",
    )(te, tv, src, src, dst, ffn_g, wg, wu, wd, x)

    tc = _pick_tile(t, (512, 256, 128))
    blocks = t // tc
    return pl.pallas_call(
        _moe_combine_kernel,
        grid=(blocks,),
        in_specs=[pl.BlockSpec((tc, d), lambda i: (i, 0)), pl.BlockSpec((tc, TOP_K), lambda i: (i, 0))]
        + [pl.BlockSpec((tc, d), functools.partial(lambda i, kk: (kk * blocks + i, 0), kk=kk)) for kk in range(TOP_K)],
        out_specs=pl.BlockSpec((tc, d), lambda i: (i, 0)),
        out_shape=jax.ShapeDtypeStruct((t, d), F32),
        compiler_params=_params(1, 2 * 4 * tc * d * 4 + 4 * MIB),
        name="moe_combine",
    )(x, gate.T, *([out2] * TOP_K))


def _moe_combine_kernel(x_ref, g_ref, *refs):
    *y_refs, o_ref = refs
    acc = x_ref[...]
    g = g_ref[...]
    for kk, y_ref in enumerate(y_refs):
        acc = acc + g[:, kk:kk + 1] * y_ref[...]
    o_ref[...] = acc


def kernel(x_prompt, x_sample, cache_k, cache_v, state_pool, state_conv, attn_norm_g, w_in, q_norm_g, k_norm_g,
           lambda_q1, lambda_k1, lambda_q2, lambda_k2, subln_g, pool_w, pool_scale, conv_w, conv_b, conv_ln_g,
           conv_ln_b, conv_pw, w_out, ffn_norm_g, ffn_w_gate, ffn_w_up, ffn_w_down, router_w, moe_w_gate,
           moe_w_up, moe_w_down):
    bp, sp, d = x_prompt.shape
    db, ds, _ = x_sample.shape
    depth = w_in.shape[0]
    past = cache_k.shape[2]
    n_sub = cache_k.shape[3]
    n_heads = n_sub // 2
    aw = n_heads * V_DIM
    pw = state_pool.shape[-1]
    cw = state_conv.shape[-1]
    pool_rows, conv_rows = state_pool.shape[2], state_conv.shape[2]
    tp, ts = bp * sp, db * ds
    assert pool_rows == max(POOL_WINDOWS) - 1 and conv_rows == CONV_K - 1
    assert w_in.shape[2] == 3 * aw + pw + 2 * cw and ds >= conv_rows and tp % ds == 0

    x = jnp.concatenate([x_prompt.reshape(tp, d), x_sample.reshape(ts, d)], axis=0)
    ck = cache_k.reshape(depth, db, past, aw)
    cv = cache_v.reshape(depth, db, past, aw)

    lane_group = jnp.arange(aw, dtype=I32) // HEAD_DIM
    gsum = (lane_group[:, None] == lane_group[None, :]).astype(BF16)
    n_groups = len(POOL_WINDOWS)
    pg = pw // n_groups
    tm_pool = _pick_tile(sp, (512, 256, 128))

    outs = {name: [] for name in ("kp", "vp", "pp", "cp", "ks", "vs", "ps", "cs")}
    for l in range(depth):
        lam_init = 0.8 - 0.6 * math.exp(-0.3 * l)
        lam = (jnp.exp(jnp.sum(lambda_q1[l] * lambda_k1[l])) - jnp.exp(jnp.sum(lambda_q2[l] * lambda_k2[l]))
               + lam_init).astype(F32).reshape(1)
        qg = (jnp.tile(q_norm_g[l], n_sub) * (HEAD_DIM ** -0.5 * math.log2(math.e))).reshape(1, aw)
        kg = jnp.tile(k_norm_g[l], n_sub).reshape(1, aw)
        sub_gain = (subln_g[l] * (1.0 - lam_init)).reshape(1, V_DIM)

        (qa, qb, kb, vx, u, g), (kp, vp, ks, vs) = _in_proj(
            x, attn_norm_g[l].reshape(1, d), w_in[l].astype(BF16), gsum, qg, kg, tp=tp, aw=aw, pw=pw, cw=cw)

        score_bound = HEAD_DIM * jnp.max(jnp.abs(qg)) * jnp.max(jnp.abs(kg)) * 1.02
        att_p = _attn_prompt(lam, qa, qb, kb, vx, sub_gain, score_bound, n_heads=n_heads, b=bp, s=sp)
        att_s = _attn_sample(lam, qa, qb, kb, vx, ck, cv, sub_gain, layer=l, n_heads=n_heads, ds=ds, row0=tp)

        pool_wbd = jax.scipy.linalg.block_diag(*[pool_w[l, gi] for gi in range(n_groups)]).astype(BF16)
        mix_w = (pool_wbd, pool_scale[l].reshape(1, pw), conv_w[l], conv_b[l].reshape(1, cw),
                 conv_ln_g[l].reshape(1, cw), conv_ln_b[l].reshape(1, cw), conv_pw[l].astype(BF16))
        zeros_p = lambda halo, c: jnp.zeros((bp, halo, c), F32)
        pool_p, conv_p = _pool_conv(
            u, g, _tile_halos(u[:tp], bp, sp, tm_pool, POOL_HALO, zeros_p(POOL_HALO, pw)),
            _tile_halos(g[:tp], bp, sp, tm_pool, CONV_HALO, zeros_p(CONV_HALO, cw)), *mix_w,
            row0=0, n_rows=tp, tm=tm_pool, tiles_per_seq=sp // tm_pool, pos0=0)
        pool_s, conv_s = _pool_conv(
            u, g, _front_pad(state_pool[l], POOL_HALO), _front_pad(state_conv[l], CONV_HALO), *mix_w,
            row0=tp, n_rows=ts, tm=ds, tiles_per_seq=1, pos0=past)
        mixed = ((att_p, pool_p, conv_p), (att_s, pool_s, conv_s))

        w_o = w_out[l].astype(BF16)
        fg = ffn_norm_g[l].reshape(1, d)
        i = l // 2
        if l % 2 == 0:
            x, h = _out_proj(*mixed, x, w_o, fg)
            x = _ffn(h, x, ffn_w_gate[i].astype(BF16), ffn_w_up[i].astype(BF16), ffn_w_down[i].astype(BF16))
        else:
            x, idx, gate, rank, counts = _out_proj(*mixed, x, w_o, fg, router=_split_bf16(router_w[i].T))
            x = _moe(x, idx, gate, rank, counts[:, 0], fg, moe_w_gate[i].astype(BF16), moe_w_up[i].astype(BF16),
                     moe_w_down[i].astype(BF16))

        outs["kp"].append(kp.reshape(bp, sp, n_sub, HEAD_DIM))
        outs["vp"].append(vp.reshape(bp, sp, n_heads, V_DIM))
        outs["pp"].append(u[:tp].reshape(bp, sp, pw)[:, sp - pool_rows:])
        outs["cp"].append(g[:tp].reshape(bp, sp, cw)[:, sp - conv_rows:])
        outs["ks"].append(ks.reshape(db, ds, n_sub, HEAD_DIM))
        outs["vs"].append(vs.reshape(db, ds, n_heads, V_DIM))
        outs["ps"].append(u[tp:].reshape(db, ds, pw)[:, ds - pool_rows:])
        outs["cs"].append(g[tp:].reshape(db, ds, cw)[:, ds - conv_rows:])

    stack = lambda name: jnp.stack(outs[name])
    return (x[:tp].reshape(bp, sp, d), x[tp:].reshape(db, ds, d),
            stack("kp"), stack("vp"), stack("pp"), stack("cp"),
            stack("ks"), stack("vs"), stack("ps"), stack("cs"))
```

```python
import functools
import math

import jax
import jax.numpy as jnp
from jax import lax
from jax.experimental import pallas as pl
from jax.experimental.pallas import tpu as pltpu

F32 = jnp.float32
BF16 = jnp.bfloat16
I32 = jnp.int32

EPS = 1e-6
CHUNK = 64
HEAD_DIM = 64
V_DIM = 2 * HEAD_DIM
POOL_WINDOWS = (2, 4, 8, 16)
POOL_HALO = 16
CONV_K = 31
CONV_HALO = 32
TOP_K = 2
MOE_TILE_ROWS = 256
NEG_BIG = -1e30

LANES = 128
MIB = 1024 * 1024
VMEM_CAP_BYTES = 60 * MIB


def _pick_tile(n, candidates):
    for c in candidates:
        if n % c == 0:
            return c
    raise ValueError(f"no tile in {candidates} divides {n}")


def _params(n_grid_dims, vmem_bytes):
    return pltpu.CompilerParams(dimension_semantics=("arbitrary",) * n_grid_dims,
                                vmem_limit_bytes=int(min(vmem_bytes, VMEM_CAP_BYTES)))


def _nt_dot(a, b):
    return lax.dot_general(a, b, (((1,), (1,)), ((), ())), preferred_element_type=F32)


def _dot(a, b):
    return jnp.dot(a, b, preferred_element_type=F32)


def _split_bf16(x):
    hi = x.astype(BF16)
    lo = (x - hi.astype(F32)).astype(BF16)
    return hi, lo


def _rms_rows(x, gain):
    ms = jnp.mean(x * x, axis=-1, keepdims=True)
    return x * lax.rsqrt(ms + EPS) * gain


def _in_proj_kernel(x_ref, ng_ref, w_ref, gsum_ref, qg_ref, kg_ref,
                    qa_ref, qb_ref, kb_ref, vx_ref, u_ref, g_ref, kp_ref, vp_ref, ks_ref, vs_ref,
                    *, aw, pw, cw, n_prompt_tiles):
    h = _rms_rows(x_ref[...], ng_ref[...]).astype(BF16)

    def proj(lo, width):
        return _dot(h, w_ref[:, lo:lo + width])

    def head_norm(t, gain):
        ss = _dot((t * t).astype(BF16), gsum_ref[...])
        return t * lax.rsqrt(ss * (1.0 / HEAD_DIM) + EPS) * gain

    q = head_norm(proj(0, aw), qg_ref[...])
    lane = lax.broadcasted_iota(I32, q.shape, 1)
    first_map = (lane // HEAD_DIM) % 2 == 0
    qa_ref[...] = jnp.where(first_map, q, 0.0).astype(BF16)
    qb_ref[...] = jnp.where(first_map, 0.0, q).astype(BF16)

    k = head_norm(proj(aw, aw), kg_ref[...])
    kb_ref[...] = k.astype(BF16)
    v = proj(2 * aw, aw)
    vb = v.astype(BF16)
    ones = jnp.ones((vb.shape[0], V_DIM), BF16)
    for hh in range(aw // V_DIM):
        vx_ref[:, 2 * hh * V_DIM:(2 * hh + 1) * V_DIM] = vb[:, hh * V_DIM:(hh + 1) * V_DIM]
        vx_ref[:, (2 * hh + 1) * V_DIM:(2 * hh + 2) * V_DIM] = ones

    is_prompt = pl.program_id(0) < n_prompt_tiles

    @pl.when(is_prompt)
    def _():
        kp_ref[...] = k
        vp_ref[...] = v

    @pl.when(jnp.logical_not(is_prompt))
    def _():
        ks_ref[...] = k
        vs_ref[...] = v

    u_ref[...] = proj(3 * aw, pw)
    a = proj(3 * aw + pw, cw)
    gate = proj(3 * aw + pw + cw, cw)
    g_ref[...] = a * jax.nn.sigmoid(gate)


def _in_proj(x, norm_g, w_in, gsum, qg, kg, *, tp, aw, pw, cw):
    t, d = x.shape
    ts = t - tp
    tm = _pick_tile(math.gcd(tp, ts), (512, 256, 128))
    npt = tp // tm
    in_w = w_in.shape[1]
    row = lambda w: pl.BlockSpec((tm, w), lambda i: (i, 0))
    full = lambda a: pl.BlockSpec(a.shape, lambda i: (0,) * a.ndim)
    out_widths = (aw, aw, aw, 2 * aw, pw, cw)
    out_dtypes = (BF16, BF16, BF16, BF16, F32, F32)
    rows_p = pl.BlockSpec((tm, aw), lambda i: (jnp.minimum(i, npt - 1), 0))
    rows_s = pl.BlockSpec((tm, aw), lambda i: (jnp.maximum(i - npt, 0), 0))
    vmem = 2 * (tm * d * 4 + d * in_w * 2 + 4 * tm * aw * 4
                + sum(tm * w * jnp.dtype(dt).itemsize for w, dt in zip(out_widths, out_dtypes)))
    vmem += 8 * tm * aw * 4 + 2 * aw * aw * 2
    outs = pl.pallas_call(
        functools.partial(_in_proj_kernel, aw=aw, pw=pw, cw=cw, n_prompt_tiles=npt),
        grid=(t // tm,),
        in_specs=[row(d), full(norm_g), full(w_in), full(gsum), full(qg), full(kg)],
        out_specs=[row(w) for w in out_widths] + [rows_p, rows_p, rows_s, rows_s],
        out_shape=[jax.ShapeDtypeStruct((t, w), dt) for w, dt in zip(out_widths, out_dtypes)]
        + [jax.ShapeDtypeStruct((rows, aw), F32) for rows in (tp, tp, ts, ts)],
        compiler_params=_params(1, vmem + 8 * MIB),
        name="in_proj",
    )(x, norm_g, w_in, gsum, qg, kg)
    return outs[:len(out_widths)], outs[len(out_widths):]


def _diff_finish(o, lam, sub_gain, rows):
    a = o[:rows] - lam * o[rows:]
    return _rms_rows(a, sub_gain)


def _attn_prompt_kernel(lam_ref, qa_ref, qb_ref, k_ref, vx_ref, sg_ref, o_ref,
                        q_sc, s0_sc, s1_sc, p0_sc, p1_sc, a0_sc, a1_sc, m_sc, acc_sc, *, tq):
    i = pl.program_id(2)
    tk = tq
    q_sc[0:tq] = qa_ref[...]
    q_sc[tq:2 * tq] = qb_ref[...]
    m_sc[...] = jnp.full(m_sc.shape, NEG_BIG, F32)
    acc_sc[...] = jnp.zeros(acc_sc.shape, F32)
    p1_sc[...] = jnp.zeros(p1_sc.shape, BF16)
    a1_sc[...] = jnp.ones(a1_sc.shape, F32)
    even = (s0_sc, p0_sc, a0_sc)
    odd = (s1_sc, p1_sc, a1_sc)

    def rows(j):
        return pl.ds(pl.multiple_of(j * tk, tk), tk)

    def scores(j):
        return _nt_dot(q_sc[...], k_ref[rows(j), :])

    def softmax(buf, masked):
        s_ref, p_ref, a_ref = buf
        s = s_ref[...]
        if masked:
            q_chunk = (lax.broadcasted_iota(I32, s.shape, 0) % tq) // CHUNK
            k_chunk = lax.broadcasted_iota(I32, s.shape, 1) // CHUNK
            s = jnp.where(k_chunk <= q_chunk, s, NEG_BIG)
        m_old = m_sc[...]
        m_new = jnp.maximum(m_old, jnp.max(s, axis=-1, keepdims=True))
        a_ref[...] = jnp.exp2(m_old - m_new)
        p_ref[...] = jnp.exp2(s - jnp.tile(m_new, (1, tk // LANES))).astype(BF16)
        m_sc[...] = m_new

    def values(buf, j):
        _, p_ref, a_ref = buf
        acc_sc[...] = (acc_sc[...] * jnp.tile(a_ref[...], (1, 2 * V_DIM // LANES))
                       + _dot(p_ref[...], vx_ref[rows(j), :]))

    def step(cur, nxt, j):
        softmax(cur, masked=False)
        nxt[0][...] = scores(j + 1)
        values(nxt, jnp.maximum(j - 1, 0))

    def by_parity(n, fn):
        pl.when(n % 2 == 0)(lambda: fn(even, odd))
        pl.when(n % 2 == 1)(lambda: fn(odd, even))

    s0_sc[...] = scores(0)

    def body(j, carry):
        by_parity(j, lambda cur, nxt: step(cur, nxt, j))
        return carry

    lax.fori_loop(0, i, body, 0)

    def boundary_tile(cur, prv):
        softmax(cur, masked=True)
        values(prv, jnp.maximum(i - 1, 0))

    by_parity(i, boundary_tile)
    by_parity(i, lambda cur, prv: values(cur, i))

    acc = acc_sc[...]
    o = acc[:, :V_DIM] / acc[:, V_DIM:]
    o_ref[...] = _diff_finish(o, lam_ref[0], sg_ref[...], tq).astype(BF16)


def _attn_prompt_bounded_kernel(lam_ref, qa_ref, qb_ref, k_ref, vx_ref, sg_ref, o_ref,
                                q_sc, p0_sc, p1_sc, acc_sc, *, tq, tk):
    i = pl.program_id(2)
    q_sc[0:tq] = qa_ref[...]
    q_sc[tq:2 * tq] = qb_ref[...]
    acc_sc[...] = jnp.zeros(acc_sc.shape, F32)

    def rows(j):
        return pl.ds(pl.multiple_of(j * tk, tk), tk)

    def probs(j, boundary_half=None):
        s = _nt_dot(q_sc[...], k_ref[rows(j), :])
        if boundary_half is not None:
            q_chunk = (lax.broadcasted_iota(I32, s.shape, 0) % tq) // CHUNK
            k_chunk = (boundary_half * tk + lax.broadcasted_iota(I32, s.shape, 1)) // CHUNK
            s = jnp.where(k_chunk <= q_chunk, s, NEG_BIG)
        return jnp.exp2(s).astype(BF16)

    def values(p_ref, j):
        acc_sc[...] += _dot(p_ref[...], vx_ref[rows(j), :])

    def step(cur, nxt, j):
        nxt[...] = probs(j + 1)
        values(cur, j)

    def body(j, carry):
        pl.when(j % 2 == 0)(lambda: step(p0_sc, p1_sc, j))
        pl.when(j % 2 == 1)(lambda: step(p1_sc, p0_sc, j))
        return carry

    @pl.when(i > 0)
    def _():
        p0_sc[...] = probs(0)

    lax.fori_loop(0, jnp.maximum(2 * i - 1, 0), body, 0)

    @pl.when(i > 0)
    def _():
        p0_sc[...] = probs(2 * i, boundary_half=0)
        values(p1_sc, 2 * i - 1)

    @pl.when(i == 0)
    def _():
        p0_sc[...] = probs(0, boundary_half=0)

    own_region = pl.when(i >= 0)

    @own_region
    def _():
        p1_sc[...] = probs(2 * i + 1, boundary_half=1)
        values(p0_sc, 2 * i)

    @own_region
    def _():
        values(p1_sc, 2 * i + 1)

    acc = acc_sc[...]
    o = acc[:, :V_DIM] / acc[:, V_DIM:]
    o_ref[...] = _diff_finish(o, lam_ref[0], sg_ref[...], tq).astype(BF16)


SCORE_BOUND_FOR_PLAIN_EXP2 = 64.0


def _attn_prompt(lam, qa, qb, kb, vx, sub_gain, score_bound, *, n_heads, b, s):
    resident = pl.Buffered(1)

    def call(kern, tq, tk, scratch, name):
        nq = s // tq
        qspec = pl.BlockSpec((tq, V_DIM), lambda bi, hi, qi: (bi * nq + qi, hi))
        buf = {"score": pltpu.VMEM((2 * tq, tk), F32), "prob": pltpu.VMEM((2 * tq, tk), BF16),
               "row": pltpu.VMEM((2 * tq, LANES), F32), "q": pltpu.VMEM((2 * tq, V_DIM), BF16),
               "acc": pltpu.VMEM((2 * tq, 2 * V_DIM), F32)}
        vmem = s * V_DIM * 2 + s * 2 * V_DIM * 2 + 2 * (2 * tq * V_DIM * 2 + tq * V_DIM * 2)
        vmem += 2 * tq * (V_DIM * 2 + 3 * LANES * 4 + 2 * V_DIM * 4) + (2 * 4 + 2 * 2 + 3 * 4) * 2 * tq * tk
        return pl.pallas_call(
            kern,
            grid=(b, n_heads, nq),
            in_specs=[pl.BlockSpec(memory_space=pltpu.SMEM), qspec, qspec,
                      pl.BlockSpec((s, V_DIM), lambda bi, hi, qi: (bi, hi), pipeline_mode=resident),
                      pl.BlockSpec((s, 2 * V_DIM), lambda bi, hi, qi: (bi, hi), pipeline_mode=resident),
                      pl.BlockSpec((1, V_DIM), lambda bi, hi, qi: (0, 0))],
            out_specs=qspec,
            out_shape=jax.ShapeDtypeStruct((b * s, n_heads * V_DIM), BF16),
            scratch_shapes=[buf[n] for n in scratch],
            compiler_params=_params(3, vmem + 8 * MIB),
            name=name,
        )

    tq_b = _pick_tile(s, (1024, 512, 256))
    bounded = call(functools.partial(_attn_prompt_bounded_kernel, tq=tq_b, tk=tq_b // 2), tq_b, tq_b // 2,
                   ("q", "prob", "prob", "acc"), "attn_prompt_bounded")
    tq_g = _pick_tile(s, (512, 256, 128))
    general = call(functools.partial(_attn_prompt_kernel, tq=tq_g), tq_g, tq_g,
                   ("q", "score", "score", "prob", "prob", "row", "row", "row", "acc"), "attn_prompt")
    return lax.cond(score_bound < SCORE_BOUND_FOR_PLAIN_EXP2, bounded, general, lam, qa, qb, kb, vx, sub_gain)


def _attn_sample_kernel(lam_ref, qa_ref, qb_ref, kn_ref, vn_ref, ck_ref, cv_ref, sg_ref, o_ref,
                        *, n_heads, ds):
    for hh in range(n_heads):
        cols = slice(hh * V_DIM, (hh + 1) * V_DIM)
        q2 = jnp.concatenate([qa_ref[:, cols], qb_ref[:, cols]], axis=0)
        s_past = _nt_dot(q2, ck_ref[0, 0, :, cols].astype(BF16))
        s_new = _nt_dot(q2, kn_ref[:, cols])
        m = jnp.maximum(jnp.max(s_past, axis=-1, keepdims=True),
                        jnp.max(s_new, axis=-1, keepdims=True))
        p_past = jnp.exp2(s_past - m)
        p_new = jnp.exp2(s_new - m)
        denom = jnp.sum(p_past, axis=-1, keepdims=True) + jnp.sum(p_new, axis=-1, keepdims=True)
        pv = (_dot(p_past.astype(BF16), cv_ref[0, 0, :, cols].astype(BF16))
              + _dot(p_new.astype(BF16), vn_ref[:, 2 * hh * V_DIM:(2 * hh + 1) * V_DIM]))
        o_ref[:, cols] = _diff_finish(pv / denom, lam_ref[0], sg_ref[...], ds).astype(BF16)


def _attn_sample(lam, qa, qb, kb, vx, cache_k, cache_v, sub_gain, *, layer, n_heads, ds, row0):
    _, db, past, aw = cache_k.shape
    blk0 = row0 // ds
    new = lambda w: pl.BlockSpec((ds, w), lambda d: (blk0 + d, 0))
    cache = pl.BlockSpec((1, 1, past, aw), lambda d: (layer, d, 0, 0))
    vmem = 2 * (2 * past * aw * 4 + 6 * ds * aw * 2) + 8 * 2 * ds * past * 4 + 2 * past * V_DIM * 2
    return pl.pallas_call(
        functools.partial(_attn_sample_kernel, n_heads=n_heads, ds=ds),
        grid=(db,),
        in_specs=[pl.BlockSpec(memory_space=pltpu.SMEM), new(aw), new(aw), new(aw), new(2 * aw),
                  cache, cache, pl.BlockSpec((1, V_DIM), lambda d: (0, 0))],
        out_specs=pl.BlockSpec((ds, aw), lambda d: (d, 0)),
        out_shape=jax.ShapeDtypeStruct((db * ds, aw), BF16),
        compiler_params=_params(1, vmem + 8 * MIB),
        name="attn_sample",
    )(lam, qa, qb, kb, vx, cache_k, cache_v, sub_gain)


def _pool_conv_kernel(u_ref, g_ref, hu_ref, hg_ref, pw_ref, ps_ref, cw_ref, cb_ref, lg_ref, lb_ref, cpw_ref,
                      pool_ref, conv_ref, pu_sc, pg_sc, *, tm, tiles_per_seq, pos0, sub):
    i = pl.program_id(0)
    u = u_ref[...]
    pu_sc[0:POOL_HALO] = hu_ref[0]
    pu_sc[POOL_HALO:POOL_HALO + tm] = u
    pg_sc[0:CONV_HALO] = hg_ref[0]
    pg_sc[CONV_HALO:CONV_HALO + tm] = g_ref[...]

    padded = pu_sc[...]
    sums = []
    acc, width = padded, 1
    for win in POOL_WINDOWS:
        while width < win:
            acc = acc + pltpu.roll(acc, width, axis=0)
            width *= 2
        sums.append(acc[POOL_HALO:POOL_HALO + tm])
    c = u.shape[1]
    group = lax.broadcasted_iota(I32, (tm, c), 1) // (c // len(POOL_WINDOWS))
    pos = pos0 + (i % tiles_per_seq) * tm + lax.broadcasted_iota(I32, (tm, c), 0)
    tot, win_lane = sums[-1], jnp.full((tm, c), POOL_WINDOWS[-1], I32)
    for gi in range(len(POOL_WINDOWS) - 2, -1, -1):
        tot = jnp.where(group == gi, sums[gi], tot)
        win_lane = jnp.where(group == gi, POOL_WINDOWS[gi], win_lane)
    cnt = jnp.minimum(pos + 1, win_lane).astype(F32)
    d = tot / cnt - u
    pool_ref[...] = (_dot(d.astype(BF16), pw_ref[...]) * ps_ref[...]).astype(BF16)

    for r0 in range(0, tm, sub):
        y = jnp.zeros((sub, c), F32)
        for j in range(CONV_K):
            lo = r0 + CONV_HALO - (CONV_K - 1) + j
            y = y + pg_sc[lo:lo + sub, :] * cw_ref[j:j + 1, :]
        y = y + cb_ref[...]
        mu = jnp.mean(y, axis=-1, keepdims=True)
        yc = y - mu
        yn = yc * lax.rsqrt(jnp.mean(yc * yc, axis=-1, keepdims=True) + EPS) * lg_ref[...] + lb_ref[...]
        act = yn * jax.nn.sigmoid(yn)
        conv_ref[r0:r0 + sub, :] = _dot(act.astype(BF16), cpw_ref[...]).astype(BF16)


def _pool_conv(u, g, halo_u, halo_g, pool_wbd, pool_scale, conv_w, conv_b, ln_g, ln_b, conv_pw,
               *, row0, n_rows, tm, tiles_per_seq, pos0):
    c = u.shape[1]
    n_tiles = n_rows // tm
    blk0 = row0 // tm
    sub = _pick_tile(tm, (64, 32))
    row_in = pl.BlockSpec((tm, c), lambda i: (blk0 + i, 0))
    row_out = pl.BlockSpec((tm, c), lambda i: (i, 0))
    full = lambda a: pl.BlockSpec(a.shape, lambda i: (0,) * a.ndim)
    vmem = 2 * (2 * tm * c * 4 + 2 * tm * c * 2) + 14 * (tm + CONV_HALO) * c * 4 + 4 * c * c * 2
    return pl.pallas_call(
        functools.partial(_pool_conv_kernel, tm=tm, tiles_per_seq=tiles_per_seq, pos0=pos0, sub=sub),
        grid=(n_tiles,),
        in_specs=[row_in, row_in,
                  pl.BlockSpec((1, POOL_HALO, c), lambda i: (i, 0, 0)),
                  pl.BlockSpec((1, CONV_HALO, c), lambda i: (i, 0, 0)),
                  full(pool_wbd), full(pool_scale), full(conv_w), full(conv_b), full(ln_g), full(ln_b),
                  full(conv_pw)],
        out_specs=[row_out, row_out],
        out_shape=[jax.ShapeDtypeStruct((n_rows, c), BF16)] * 2,
        scratch_shapes=[pltpu.VMEM((tm + POOL_HALO, c), F32), pltpu.VMEM((tm + CONV_HALO, c), F32)],
        compiler_params=_params(1, vmem + 8 * MIB),
        name="pool_conv",
    )(u, g, halo_u, halo_g, pool_wbd, pool_scale, conv_w, conv_b, ln_g, ln_b, conv_pw)


def _tile_halos(rows, n_seq, seq_len, tm, halo, first):
    c = rows.shape[1]
    tiles = seq_len // tm
    tails = rows.reshape(n_seq, tiles, tm, c)[:, :tiles - 1, tm - halo:, :]
    return jnp.concatenate([first[:, None], tails], axis=1).reshape(n_seq * tiles, halo, c)


def _front_pad(state, halo):
    n, rows, c = state.shape
    return jnp.concatenate([jnp.zeros((n, halo - rows, c), state.dtype), state], axis=1)


def _mix_out(mixer_refs, x_ref, w_ref, xo_ref, n_prompt_tiles):
    def mix(att_ref, pool_ref, conv_ref):
        aw, pw = att_ref.shape[1], pool_ref.shape[1]
        xo_ref[...] = (x_ref[...] + _dot(att_ref[...], w_ref[0:aw, :]) + _dot(pool_ref[...], w_ref[aw:aw + pw, :])
                       + _dot(conv_ref[...], w_ref[aw + pw:, :]))

    is_prompt = pl.program_id(0) < n_prompt_tiles
    pl.when(is_prompt)(lambda: mix(*mixer_refs[:3]))
    pl.when(jnp.logical_not(is_prompt))(lambda: mix(*mixer_refs[3:]))
    return xo_ref[...]


def _out_proj_dense_kernel(*refs, n_prompt_tiles):
    mixer_refs, (x_ref, w_ref, fg_ref, xo_ref, h_ref) = refs[:6], refs[6:]
    x = _mix_out(mixer_refs, x_ref, w_ref, xo_ref, n_prompt_tiles)
    h_ref[...] = _rms_rows(x, fg_ref[...]).astype(BF16)


def _out_proj_router_kernel(*refs, n_prompt_tiles):
    mixer_refs = refs[:6]
    (x_ref, w_ref, fg_ref, rh_ref, rl_ref, tri_ref,
     xo_ref, idx_ref, gate_ref, rank_ref, cnt_ref, carry_sc) = refs[6:]

    @pl.when(pl.program_id(0) == 0)
    def _():
        carry_sc[...] = jnp.zeros(carry_sc.shape, F32)

    x = _mix_out(mixer_refs, x_ref, w_ref, xo_ref, n_prompt_tiles)
    h_hi, h_lo = _split_bf16(_rms_rows(x, fg_ref[...]))
    logits = _nt_dot(rh_ref[...], h_hi) + _nt_dot(rh_ref[...], h_lo) + _nt_dot(rl_ref[...], h_hi)
    n_exp = logits.shape[0]
    eidx = lax.broadcasted_iota(I32, logits.shape, 0)
    m1 = jnp.max(logits, axis=0, keepdims=True)
    i1 = jnp.min(jnp.where(logits == m1, eidx, n_exp), axis=0, keepdims=True)
    rest = jnp.where(eidx == i1, -jnp.inf, logits)
    m2 = jnp.max(rest, axis=0, keepdims=True)
    i2 = jnp.min(jnp.where(rest == m2, eidx, n_exp), axis=0, keepdims=True)
    e2 = jnp.exp(m2 - m1)
    idx_ref[...] = jnp.concatenate([i1, i2], axis=0)
    gate_ref[...] = jnp.concatenate([1.0 / (1.0 + e2), e2 / (1.0 + e2)], axis=0)

    pick1, pick2 = eidx == i1, eidx == i2
    picks = jnp.concatenate([pick1, pick2], axis=0).astype(BF16)
    earlier = _dot(picks, tri_ref[...])
    tot1 = jnp.sum(pick1.astype(F32), axis=1, keepdims=True)
    tot2 = jnp.sum(pick2.astype(F32), axis=1, keepdims=True)
    carry = carry_sc[...]
    base = carry[:, 0:1]
    r1 = jnp.sum(jnp.where(pick1, base + earlier[:n_exp], 0.0), axis=0, keepdims=True)
    r2 = jnp.sum(jnp.where(pick2, base + tot1 + earlier[n_exp:], 0.0), axis=0, keepdims=True)
    rank_ref[...] = jnp.concatenate([r1, r2], axis=0).astype(I32)
    carry = carry + tot1 + tot2
    carry_sc[...] = carry
    cnt_ref[...] = carry.astype(I32)


def _out_proj(mix_prompt, mix_sample, x, w_out, ffn_g, router=None):
    t, d = x.shape
    tp = mix_prompt[0].shape[0]
    tm = _pick_tile(math.gcd(tp, t - tp), (512, 256, 128))
    npt = tp // tm
    row = lambda a: pl.BlockSpec((tm, a.shape[1]), lambda i: (i, 0))
    rows_p = lambda a: pl.BlockSpec((tm, a.shape[1]), lambda i: (jnp.minimum(i, npt - 1), 0))
    rows_s = lambda a: pl.BlockSpec((tm, a.shape[1]), lambda i: (jnp.maximum(i - npt, 0), 0))
    full = lambda a: pl.BlockSpec(a.shape, lambda i: (0,) * a.ndim)
    ins = list(mix_prompt) + list(mix_sample) + [x, w_out, ffn_g]
    in_specs = [rows_p(a) for a in mix_prompt] + [rows_s(a) for a in mix_sample] + [row(x), full(w_out), full(ffn_g)]
    vmem = 2 * (2 * tm * d * 2 + tm * d * 4 * 2 + d * d * 2 + tm * d * 2) + 6 * tm * d * 4
    scratch = []
    if router is None:
        kern = _out_proj_dense_kernel
        out_specs = [row(x), row(x)]
        out_shape = [jax.ShapeDtypeStruct((t, d), F32), jax.ShapeDtypeStruct((t, d), BF16)]
    else:
        kern = _out_proj_router_kernel
        n_exp = router[0].shape[0]
        earlier_token = jnp.arange(tm, dtype=I32)[:, None] < jnp.arange(tm, dtype=I32)[None, :]
        ins += list(router) + [earlier_token.astype(BF16)]
        in_specs += [full(router[0]), full(router[1]), pl.BlockSpec((tm, tm), lambda i: (0, 0))]
        top = pl.BlockSpec((TOP_K, tm), lambda i: (0, i))
        out_specs = [row(x), top, top, top, pl.BlockSpec((n_exp, LANES), lambda i: (0, 0))]
        out_shape = [jax.ShapeDtypeStruct((t, d), F32), jax.ShapeDtypeStruct((TOP_K, t), I32),
                     jax.ShapeDtypeStruct((TOP_K, t), F32), jax.ShapeDtypeStruct((TOP_K, t), I32),
                     jax.ShapeDtypeStruct((n_exp, LANES), I32)]
        scratch = [pltpu.VMEM((n_exp, LANES), F32)]
        vmem += 2 * tm * tm * 2
    return pl.pallas_call(
        functools.partial(kern, n_prompt_tiles=npt),
        grid=(t // tm,), in_specs=in_specs, out_specs=out_specs, out_shape=out_shape,
        scratch_shapes=scratch, compiler_params=_params(1, vmem + 8 * MIB), name="out_proj",
    )(*ins)


def _swiglu_acc(h, acc, wg_ref, wu_ref, wd_ref, lead, tf):
    f = wg_ref.shape[-1]
    for c0 in range(0, f, tf):
        gate = _dot(h, wg_ref[lead + (slice(None), slice(c0, c0 + tf))])
        up = _dot(h, wu_ref[lead + (slice(None), slice(c0, c0 + tf))])
        act = (gate * jax.nn.sigmoid(gate) * up).astype(BF16)
        acc = acc + _dot(act, wd_ref[lead + (slice(c0, c0 + tf), slice(None))])
    return acc


def _ffn_kernel(h_ref, x_ref, wg_ref, wu_ref, wd_ref, o_ref, *, tf):
    o_ref[...] = _swiglu_acc(h_ref[...], x_ref[...], wg_ref, wu_ref, wd_ref, (), tf)


def _ffn(h, x, wg, wu, wd):
    t, d = x.shape
    f = wg.shape[1]
    tm = _pick_tile(t, (512, 256, 128))
    tf = _pick_tile(f, (512, 256, 128))
    row = lambda: pl.BlockSpec((tm, d), lambda i: (i, 0))
    full = lambda a: pl.BlockSpec(a.shape, lambda i: (0,) * a.ndim)
    vmem = 2 * (3 * d * f * 2 + tm * d * (2 + 4 + 4)) + 4 * tm * tf * 4 + 2 * tm * d * 4
    return pl.pallas_call(
        functools.partial(_ffn_kernel, tf=tf),
        grid=(t // tm,),
        in_specs=[row(), row(), full(wg), full(wu), full(wd)],
        out_specs=row(),
        out_shape=jax.ShapeDtypeStruct((t, d), F32),
        compiler_params=_params(1, vmem + 4 * MIB),
        name="ffn",
    )(h, x, wg, wu, wd)


def _moe_kernel(te_ref, tv_ref, src_ref, src_next_ref, dst_ref, fg_ref, wg_ref, wu_ref, wd_ref, x_hbm,
                out_hbm, xbuf, ybuf, gsem, ssem, *, tm, tf, n_tiles):
    i = pl.program_id(0)
    slot = i % 2

    def gather_row(idx_ref, r, s):
        return pltpu.make_async_copy(x_hbm.at[pl.ds(idx_ref[0, 0, r], 1)], xbuf.at[s, pl.ds(r, 1)], gsem.at[s])

    def scatter_row(row, r, s):
        return pltpu.make_async_copy(ybuf.at[s, pl.ds(r, 1)], out_hbm.at[pl.ds(row, 1)], ssem.at[s])

    def for_rows(fn, static=False):
        if static:
            for r in range(tm):
                fn(r)
            return

        def body(r, carry):
            fn(r)
            return carry
        lax.fori_loop(0, tm, body, 0, unroll=8)

    @pl.when(i == 0)
    def _():
        for_rows(lambda r: gather_row(src_ref, r, slot).start())

    def for_static_slot(fn, when=True):
        for s in (0, 1):
            pl.when(jnp.logical_and(when, slot == s))(functools.partial(fn, s))

    for_static_slot(lambda s: for_rows(lambda r: gather_row(src_next_ref, r, 1 - s).start(), static=True),
                    when=i + 1 < n_tiles)

    for_rows(lambda r: gather_row(src_ref, r, slot).wait())

    @pl.when(i >= 2)
    def _():
        for_rows(lambda r: scatter_row(0, r, slot).wait())

    @pl.when(tv_ref[i] > 0)
    def _():
        h = _rms_rows(xbuf[slot], fg_ref[...]).astype(BF16)
        lead = (0,)
        ybuf[slot] = _swiglu_acc(h, jnp.zeros((tm, h.shape[1]), F32), wg_ref, wu_ref, wd_ref, lead, tf)

    @pl.when(tv_ref[i] == 0)
    def _():
        ybuf[slot] = jnp.zeros(ybuf.shape[1:], F32)

    for_static_slot(lambda s: for_rows(lambda r: scatter_row(dst_ref[0, 0, r], r, s).start(), static=True))

    @pl.when(i == n_tiles - 1)
    def _():
        for_rows(lambda r: scatter_row(0, r, slot).wait())
        if n_tiles > 1:
            for_rows(lambda r: scatter_row(0, r, 1 - slot).wait())


def _moe_route(idx, rank, counts, tm):
    k, t = idx.shape
    a = k * t
    n_exp = counts.shape[0]
    padded = ((counts + tm - 1) // tm) * tm
    ends = jnp.cumsum(padded)
    offs = ends - padded
    pos = rank
    for e in range(n_exp):
        pos = pos + jnp.where(idx == e, offs[e], 0)
    n_pad = a + n_exp * tm
    n_tiles = n_pad // tm
    slot_ids = jnp.arange(n_pad, dtype=I32)
    choice = jnp.full((n_pad,), -1, I32).at[pos.reshape(a)].set(jnp.arange(a, dtype=I32))
    filled = choice >= 0
    src = jnp.where(filled, choice % t, 0)
    spare = a + ((slot_ids // tm) % 2) * tm + slot_ids % tm
    dst = jnp.where(filled, choice, spare)
    starts = jnp.arange(n_tiles, dtype=I32) * tm
    tile_expert = jnp.minimum(jnp.sum((starts[:, None] >= ends[None, :]).astype(I32), axis=1), n_exp - 1)
    tile_valid = (starts < ends[-1]).astype(I32)
    shape3 = (n_tiles, 1, tm)
    return tile_expert, tile_valid, src.reshape(shape3), dst.reshape(shape3)


def _moe(x, idx, gate, rank, counts, ffn_g, wg, wu, wd):
    t, d = x.shape
    n_exp, _, f = wg.shape
    tm = MOE_TILE_ROWS
    tf = _pick_tile(f, (512, 256, 128))
    te, tv, src, dst = _moe_route(idx, rank, counts, tm)
    n_tiles = src.shape[0]
    n_out = TOP_K * t + 2 * tm
    smem_rows = lambda nxt: pl.BlockSpec(
        (1, 1, tm), (lambda i, te, tv: (jnp.minimum(i + 1, n_tiles - 1), 0, 0)) if nxt else (lambda i, te, tv: (i, 0, 0)),
        memory_space=pltpu.SMEM)
    wspec = lambda a: pl.BlockSpec((1,) + a.shape[1:], lambda i, te, tv: (te[i], 0, 0))
    vmem = 2 * 3 * d * f * 2 + 4 * tm * d * 4 + 4 * tm * tf * 4 + 3 * tm * d * 4
    out2 = pl.pallas_call(
        functools.partial(_moe_kernel, tm=tm, tf=tf, n_tiles=n_tiles),
        grid_spec=pltpu.PrefetchScalarGridSpec(
            num_scalar_prefetch=2, grid=(n_tiles,),
            in_specs=[smem_rows(False), smem_rows(True), smem_rows(False),
                      pl.BlockSpec(ffn_g.shape, lambda i, te, tv: (0, 0)),
                      wspec(wg), wspec(wu), wspec(wd),
                      pl.BlockSpec(memory_space=pl.ANY)],
            out_specs=pl.BlockSpec(memory_space=pl.ANY),
            scratch_shapes=[pltpu.VMEM((2, tm, d), F32), pltpu.VMEM((2, tm, d), F32),
                            pltpu.SemaphoreType.DMA((2,)), pltpu.SemaphoreType.DMA((2,))]),
        out_shape=jax.ShapeDtypeStruct((n_out, d), F32),
        compiler_params=_params(1, vmem + 4 * MIB),
        name="moe_experts",
    )(te, tv, src, src, dst, ffn_g, wg, wu, wd, x)

    tc = _pick_tile(t, (512, 256, 128))
    blocks = t // tc
    return pl.pallas_call(
        _moe_combine_kernel,
        grid=(blocks,),
        in_specs=[pl.BlockSpec((tc, d), lambda i: (i, 0)), pl.BlockSpec((tc, TOP_K), lambda i: (i, 0))]
        + [pl.BlockSpec((tc, d), functools.partial(lambda i, kk: (kk * blocks + i, 0), kk=kk)) for kk in range(TOP_K)],
        out_specs=pl.BlockSpec((tc, d), lambda i: (i, 0)),
        out_shape=jax.ShapeDtypeStruct((t, d), F32),
        compiler_params=_params(1, 2 * 4 * tc * d * 4 + 4 * MIB),
        name="moe_combine",
    )(x, gate.T, *([out2] * TOP_K))


def _moe_combine_kernel(x_ref, g_ref, *refs):
    *y_refs, o_ref = refs
    acc = x_ref[...]
    g = g_ref[...]
    for kk, y_ref in enumerate(y_refs):
        acc = acc + g[:, kk:kk + 1] * y_ref[...]
    o_ref[...] = acc


def kernel(x_prompt, x_sample, cache_k, cache_v, state_pool, state_conv, attn_norm_g, w_in, q_norm_g, k_norm_g,
           lambda_q1, lambda_k1, lambda_q2, lambda_k2, subln_g, pool_w, pool_scale, conv_w, conv_b, conv_ln_g,
           conv_ln_b, conv_pw, w_out, ffn_norm_g, ffn_w_gate, ffn_w_up, ffn_w_down, router_w, moe_w_gate,
           moe_w_up, moe_w_down):
    bp, sp, d = x_prompt.shape
    db, ds, _ = x_sample.shape
    depth = w_in.shape[0]
    past = cache_k.shape[2]
    n_sub = cache_k.shape[3]
    n_heads = n_sub // 2
    aw = n_heads * V_DIM
    pw = state_pool.shape[-1]
    cw = state_conv.shape[-1]
    pool_rows, conv_rows = state_pool.shape[2], state_conv.shape[2]
    tp, ts = bp * sp, db * ds
    assert pool_rows == max(POOL_WINDOWS) - 1 and conv_rows == CONV_K - 1
    assert w_in.shape[2] == 3 * aw + pw + 2 * cw and ds >= conv_rows and tp % ds == 0

    x = jnp.concatenate([x_prompt.reshape(tp, d), x_sample.reshape(ts, d)], axis=0)
    ck = cache_k.reshape(depth, db, past, aw)
    cv = cache_v.reshape(depth, db, past, aw)

    lane_group = jnp.arange(aw, dtype=I32) // HEAD_DIM
    gsum = (lane_group[:, None] == lane_group[None, :]).astype(BF16)
    n_groups = len(POOL_WINDOWS)
    tm_pool = _pick_tile(sp, (512, 256, 128))

    outs = {name: [] for name in ("kp", "vp", "pp", "cp", "ks", "vs", "ps", "cs")}
    for l in range(depth):
        lam_init = 0.8 - 0.6 * math.exp(-0.3 * l)
        lam = (jnp.exp(jnp.sum(lambda_q1[l] * lambda_k1[l])) - jnp.exp(jnp.sum(lambda_q2[l] * lambda_k2[l]))
               + lam_init).astype(F32).reshape(1)
        qg = (jnp.tile(q_norm_g[l], n_sub) * (HEAD_DIM ** -0.5 * math.log2(math.e))).reshape(1, aw)
        kg = jnp.tile(k_norm_g[l], n_sub).reshape(1, aw)
        sub_gain = (subln_g[l] * (1.0 - lam_init)).reshape(1, V_DIM)

        (qa, qb, kb, vx, u, g), (kp, vp, ks, vs) = _in_proj(
            x, attn_norm_g[l].reshape(1, d), w_in[l].astype(BF16), gsum, qg, kg, tp=tp, aw=aw, pw=pw, cw=cw)

        score_bound = HEAD_DIM * jnp.max(jnp.abs(qg)) * jnp.max(jnp.abs(kg)) * 1.02
        att_p = _attn_prompt(lam, qa, qb, kb, vx, sub_gain, score_bound, n_heads=n_heads, b=bp, s=sp)
        att_s = _attn_sample(lam, qa, qb, kb, vx, ck, cv, sub_gain, layer=l, n_heads=n_heads, ds=ds, row0=tp)

        pool_wbd = jax.scipy.linalg.block_diag(*[pool_w[l, gi] for gi in range(n_groups)]).astype(BF16)
        mix_w = (pool_wbd, pool_scale[l].reshape(1, pw), conv_w[l], conv_b[l].reshape(1, cw),
                 conv_ln_g[l].reshape(1, cw), conv_ln_b[l].reshape(1, cw), conv_pw[l].astype(BF16))
        zeros_p = lambda halo, c: jnp.zeros((bp, halo, c), F32)
        pool_p, conv_p = _pool_conv(
            u, g, _tile_halos(u[:tp], bp, sp, tm_pool, POOL_HALO, zeros_p(POOL_HALO, pw)),
            _tile_halos(g[:tp], bp, sp, tm_pool, CONV_HALO, zeros_p(CONV_HALO, cw)), *mix_w,
            row0=0, n_rows=tp, tm=tm_pool, tiles_per_seq=sp // tm_pool, pos0=0)
        pool_s, conv_s = _pool_conv(
            u, g, _front_pad(state_pool[l], POOL_HALO), _front_pad(state_conv[l], CONV_HALO), *mix_w,
            row0=tp, n_rows=ts, tm=ds, tiles_per_seq=1, pos0=past)
        mixed = ((att_p, pool_p, conv_p), (att_s, pool_s, conv_s))

        w_o = w_out[l].astype(BF16)
        fg = ffn_norm_g[l].reshape(1, d)
        i = l // 2
        if l % 2 == 0:
            x, h = _out_proj(*mixed, x, w_o, fg)
            x = _ffn(h, x, ffn_w_gate[i].astype(BF16), ffn_w_up[i].astype(BF16), ffn_w_down[i].astype(BF16))
        else:
            x, idx, gate, rank, counts = _out_proj(*mixed, x, w_o, fg, router=_split_bf16(router_w[i].T))
            x = _moe(x, idx, gate, rank, counts[:, 0], fg, moe_w_gate[i].astype(BF16), moe_w_up[i].astype(BF16),
                     moe_w_down[i].astype(BF16))

        outs["kp"].append(kp.reshape(bp, sp, n_sub, HEAD_DIM))
        outs["vp"].append(vp.reshape(bp, sp, n_heads, V_DIM))
        outs["pp"].append(u[:tp].reshape(bp, sp, pw)[:, sp - pool_rows:])
        outs["cp"].append(g[:tp].reshape(bp, sp, cw)[:, sp - conv_rows:])
        outs["ks"].append(ks.reshape(db, ds, n_sub, HEAD_DIM))
        outs["vs"].append(vs.reshape(db, ds, n_heads, V_DIM))
        outs["ps"].append(u[tp:].reshape(db, ds, pw)[:, ds - pool_rows:])
        outs["cs"].append(g[tp:].reshape(db, ds, cw)[:, ds - conv_rows:])

    stack = lambda name: jnp.stack(outs[name])
    return (x[:tp].reshape(bp, sp, d), x[tp:].reshape(db, ds, d),
            stack("kp"), stack("vp"), stack("pp"), stack("cp"),
            stack("ks"), stack("vs"), stack("ps"), stack("cs"))
```

```python
import functools
import math

import jax
import jax.numpy as jnp
from jax import lax
from jax.experimental import pallas as pl
from jax.experimental.pallas import tpu as pltpu

F32 = jnp.float32
BF16 = jnp.bfloat16
I32 = jnp.int32

EPS = 1e-6
CHUNK = 64
HEAD_DIM = 64
V_DIM = 2 * HEAD_DIM
POOL_WINDOWS = (2, 4, 8, 16)
POOL_HALO = 16
CONV_K = 31
CONV_HALO = 32
TOP_K = 2
MOE_TILE_ROWS = 256
NEG_BIG = -1e30

LANES = 128
MIB = 1024 * 1024
VMEM_CAP_BYTES = 60 * MIB


def _pick_tile(n, candidates):
    for c in candidates:
        if n % c == 0:
            return c
    raise ValueError(f"no tile in {candidates} divides {n}")


def _params(n_grid_dims, vmem_bytes):
    return pltpu.CompilerParams(dimension_semantics=("arbitrary",) * n_grid_dims,
                                vmem_limit_bytes=int(min(vmem_bytes, VMEM_CAP_BYTES)))


def _nt_dot(a, b):
    return lax.dot_general(a, b, (((1,), (1,)), ((), ())), preferred_element_type=F32)


def _dot(a, b):
    return jnp.dot(a, b, preferred_element_type=F32)


def _split_bf16(x):
    hi = x.astype(BF16)
    lo = (x - hi.astype(F32)).astype(BF16)
    return hi, lo


def _rms_rows(x, gain):
    ms = jnp.mean(x * x, axis=-1, keepdims=True)
    return x * lax.rsqrt(ms + EPS) * gain


def _in_proj_kernel(x_ref, ng_ref, w_ref, gsum_ref, qg_ref, kg_ref,
                    qa_ref, qb_ref, kb_ref, vx_ref, u_ref, g_ref, kp_ref, vp_ref, ks_ref, vs_ref,
                    *, aw, pw, cw, n_prompt_tiles):
    h = _rms_rows(x_ref[...], ng_ref[...]).astype(BF16)

    def proj(lo, width):
        return _dot(h, w_ref[:, lo:lo + width])

    def head_norm(t, gain):
        ss = _dot((t * t).astype(BF16), gsum_ref[...])
        return t * lax.rsqrt(ss * (1.0 / HEAD_DIM) + EPS) * gain

    q = head_norm(proj(0, aw), qg_ref[...])
    lane = lax.broadcasted_iota(I32, q.shape, 1)
    first_map = (lane // HEAD_DIM) % 2 == 0
    qa_ref[...] = jnp.where(first_map, q, 0.0).astype(BF16)
    qb_ref[...] = jnp.where(first_map, 0.0, q).astype(BF16)

    k = head_norm(proj(aw, aw), kg_ref[...])
    kb_ref[...] = k.astype(BF16)
    v = proj(2 * aw, aw)
    vb = v.astype(BF16)
    ones = jnp.ones((vb.shape[0], V_DIM), BF16)
    for hh in range(aw // V_DIM):
        vx_ref[:, 2 * hh * V_DIM:(2 * hh + 1) * V_DIM] = vb[:, hh * V_DIM:(hh + 1) * V_DIM]
        vx_ref[:, (2 * hh + 1) * V_DIM:(2 * hh + 2) * V_DIM] = ones

    is_prompt = pl.program_id(0) < n_prompt_tiles

    @pl.when(is_prompt)
    def _():
        kp_ref[...] = k
        vp_ref[...] = v

    @pl.when(jnp.logical_not(is_prompt))
    def _():
        ks_ref[...] = k
        vs_ref[...] = v

    u_ref[...] = proj(3 * aw, pw)
    a = proj(3 * aw + pw, cw)
    gate = proj(3 * aw + pw + cw, cw)
    g_ref[...] = a * jax.nn.sigmoid(gate)


def _in_proj(x, norm_g, w_in, gsum, qg, kg, *, tp, aw, pw, cw):
    t, d = x.shape
    ts = t - tp
    tm = _pick_tile(math.gcd(tp, ts), (512, 256, 128))
    npt = tp // tm
    in_w = w_in.shape[1]
    row = lambda w: pl.BlockSpec((tm, w), lambda i: (i, 0))
    full = lambda a: pl.BlockSpec(a.shape, lambda i: (0,) * a.ndim)
    out_widths = (aw, aw, aw, 2 * aw, pw, cw)
    out_dtypes = (BF16, BF16, BF16, BF16, F32, F32)
    rows_p = pl.BlockSpec((tm, aw), lambda i: (jnp.minimum(i, npt - 1), 0))
    rows_s = pl.BlockSpec((tm, aw), lambda i: (jnp.maximum(i - npt, 0), 0))
    vmem = 2 * (tm * d * 4 + d * in_w * 2 + 4 * tm * aw * 4
                + sum(tm * w * jnp.dtype(dt).itemsize for w, dt in zip(out_widths, out_dtypes)))
    vmem += 8 * tm * aw * 4 + 2 * aw * aw * 2
    outs = pl.pallas_call(
        functools.partial(_in_proj_kernel, aw=aw, pw=pw, cw=cw, n_prompt_tiles=npt),
        grid=(t // tm,),
        in_specs=[row(d), full(norm_g), full(w_in), full(gsum), full(qg), full(kg)],
        out_specs=[row(w) for w in out_widths] + [rows_p, rows_p, rows_s, rows_s],
        out_shape=[jax.ShapeDtypeStruct((t, w), dt) for w, dt in zip(out_widths, out_dtypes)]
        + [jax.ShapeDtypeStruct((rows, aw), F32) for rows in (tp, tp, ts, ts)],
        compiler_params=_params(1, vmem + 8 * MIB),
        name="in_proj",
    )(x, norm_g, w_in, gsum, qg, kg)
    return outs[:len(out_widths)], outs[len(out_widths):]


def _diff_finish(o, lam, sub_gain, rows):
    a = o[:rows] - lam * o[rows:]
    return _rms_rows(a, sub_gain)


def _attn_prompt_kernel(lam_ref, qa_ref, qb_ref, k_ref, vx_ref, sg_ref, o_ref,
                        q_sc, s0_sc, s1_sc, p0_sc, p1_sc, a0_sc, a1_sc, m_sc, acc_sc, *, tq):
    i = pl.program_id(2)
    tk = tq
    q_sc[0:tq] = qa_ref[...]
    q_sc[tq:2 * tq] = qb_ref[...]
    m_sc[...] = jnp.full(m_sc.shape, NEG_BIG, F32)
    acc_sc[...] = jnp.zeros(acc_sc.shape, F32)
    p1_sc[...] = jnp.zeros(p1_sc.shape, BF16)
    a1_sc[...] = jnp.ones(a1_sc.shape, F32)
    even = (s0_sc, p0_sc, a0_sc)
    odd = (s1_sc, p1_sc, a1_sc)

    def rows(j):
        return pl.ds(pl.multiple_of(j * tk, tk), tk)

    def scores(j):
        return _nt_dot(q_sc[...], k_ref[rows(j), :])

    def softmax(buf, masked):
        s_ref, p_ref, a_ref = buf
        s = s_ref[...]
        if masked:
            q_chunk = (lax.broadcasted_iota(I32, s.shape, 0) % tq) // CHUNK
            k_chunk = lax.broadcasted_iota(I32, s.shape, 1) // CHUNK
            s = jnp.where(k_chunk <= q_chunk, s, NEG_BIG)
        m_old = m_sc[...]
        m_new = jnp.maximum(m_old, jnp.max(s, axis=-1, keepdims=True))
        a_ref[...] = jnp.exp2(m_old - m_new)
        p_ref[...] = jnp.exp2(s - jnp.tile(m_new, (1, tk // LANES))).astype(BF16)
        m_sc[...] = m_new

    def values(buf, j):
        _, p_ref, a_ref = buf
        acc_sc[...] = (acc_sc[...] * jnp.tile(a_ref[...], (1, 2 * V_DIM // LANES))
                       + _dot(p_ref[...], vx_ref[rows(j), :]))

    def step(cur, nxt, j):
        softmax(cur, masked=False)
        nxt[0][...] = scores(j + 1)
        values(nxt, jnp.maximum(j - 1, 0))

    def by_parity(n, fn):
        pl.when(n % 2 == 0)(lambda: fn(even, odd))
        pl.when(n % 2 == 1)(lambda: fn(odd, even))

    s0_sc[...] = scores(0)

    def body(j, carry):
        by_parity(j, lambda cur, nxt: step(cur, nxt, j))
        return carry

    lax.fori_loop(0, i, body, 0)

    def boundary_tile(cur, prv):
        softmax(cur, masked=True)
        values(prv, jnp.maximum(i - 1, 0))

    by_parity(i, boundary_tile)
    by_parity(i, lambda cur, prv: values(cur, i))

    acc = acc_sc[...]
    o = acc[:, :V_DIM] / acc[:, V_DIM:]
    o_ref[...] = _diff_finish(o, lam_ref[0], sg_ref[...], tq).astype(BF16)


def _attn_prompt_bounded_kernel(lam_ref, qa_ref, qb_ref, k_ref, vx_ref, sg_ref, o_ref,
                                q_sc, p0_sc, p1_sc, acc_sc, *, tq, tk):
    i = pl.program_id(2)
    q_sc[0:tq] = qa_ref[...]
    q_sc[tq:2 * tq] = qb_ref[...]
    acc_sc[...] = jnp.zeros(acc_sc.shape, F32)

    def rows(j):
        return pl.ds(pl.multiple_of(j * tk, tk), tk)

    def probs(j, boundary_half=None):
        s = _nt_dot(q_sc[...], k_ref[rows(j), :])
        if boundary_half is not None:
            q_chunk = (lax.broadcasted_iota(I32, s.shape, 0) % tq) // CHUNK
            k_chunk = (boundary_half * tk + lax.broadcasted_iota(I32, s.shape, 1)) // CHUNK
            s = jnp.where(k_chunk <= q_chunk, s, NEG_BIG)
        return jnp.exp2(s).astype(BF16)

    def values(p_ref, j):
        acc_sc[...] += _dot(p_ref[...], vx_ref[rows(j), :])

    def step(cur, nxt, j):
        nxt[...] = probs(j + 1)
        values(cur, j)

    def body(j, carry):
        pl.when(j % 2 == 0)(lambda: step(p0_sc, p1_sc, j))
        pl.when(j % 2 == 1)(lambda: step(p1_sc, p0_sc, j))
        return carry

    @pl.when(i > 0)
    def _():
        p0_sc[...] = probs(0)

    lax.fori_loop(0, jnp.maximum(2 * i - 1, 0), body, 0)

    @pl.when(i > 0)
    def _():
        p0_sc[...] = probs(2 * i, boundary_half=0)
        values(p1_sc, 2 * i - 1)

    @pl.when(i == 0)
    def _():
        p0_sc[...] = probs(0, boundary_half=0)

    own_region = pl.when(i >= 0)

    @own_region
    def _():
        p1_sc[...] = probs(2 * i + 1, boundary_half=1)
        values(p0_sc, 2 * i)

    @own_region
    def _():
        values(p1_sc, 2 * i + 1)

    acc = acc_sc[...]
    o = acc[:, :V_DIM] / acc[:, V_DIM:]
    o_ref[...] = _diff_finish(o, lam_ref[0], sg_ref[...], tq).astype(BF16)


SCORE_BOUND_FOR_PLAIN_EXP2 = 64.0


def _attn_prompt(lam, qa, qb, kb, vx, sub_gain, score_bound, *, n_heads, b, s):
    resident = pl.Buffered(1)

    def call(kern, tq, tk, scratch, name):
        nq = s // tq
        qspec = pl.BlockSpec((tq, V_DIM), lambda bi, hi, qi: (bi * nq + qi, hi))
        buf = {"score": pltpu.VMEM((2 * tq, tk), F32), "prob": pltpu.VMEM((2 * tq, tk), BF16),
               "row": pltpu.VMEM((2 * tq, LANES), F32), "q": pltpu.VMEM((2 * tq, V_DIM), BF16),
               "acc": pltpu.VMEM((2 * tq, 2 * V_DIM), F32)}
        vmem = s * V_DIM * 2 + s * 2 * V_DIM * 2 + 2 * (2 * tq * V_DIM * 2 + tq * V_DIM * 2)
        vmem += 2 * tq * (V_DIM * 2 + 3 * LANES * 4 + 2 * V_DIM * 4) + (2 * 4 + 2 * 2 + 3 * 4) * 2 * tq * tk
        return pl.pallas_call(
            kern,
            grid=(b, n_heads, nq),
            in_specs=[pl.BlockSpec(memory_space=pltpu.SMEM), qspec, qspec,
                      pl.BlockSpec((s, V_DIM), lambda bi, hi, qi: (bi, hi), pipeline_mode=resident),
                      pl.BlockSpec((s, 2 * V_DIM), lambda bi, hi, qi: (bi, hi), pipeline_mode=resident),
                      pl.BlockSpec((1, V_DIM), lambda bi, hi, qi: (0, 0))],
            out_specs=qspec,
            out_shape=jax.ShapeDtypeStruct((b * s, n_heads * V_DIM), BF16),
            scratch_shapes=[buf[n] for n in scratch],
            compiler_params=_params(3, vmem + 8 * MIB),
            name=name,
        )

    tq_b = _pick_tile(s, (1024, 512, 256))
    bounded = call(functools.partial(_attn_prompt_bounded_kernel, tq=tq_b, tk=tq_b // 2), tq_b, tq_b // 2,
                   ("q", "prob", "prob", "acc"), "attn_prompt_bounded")
    tq_g = _pick_tile(s, (512, 256, 128))
    general = call(functools.partial(_attn_prompt_kernel, tq=tq_g), tq_g, tq_g,
                   ("q", "score", "score", "prob", "prob", "row", "row", "row", "acc"), "attn_prompt")
    return lax.cond(score_bound < SCORE_BOUND_FOR_PLAIN_EXP2, bounded, general, lam, qa, qb, kb, vx, sub_gain)


def _attn_sample_kernel(lam_ref, qa_ref, qb_ref, kn_ref, vn_ref, ck_ref, cv_ref, sg_ref, o_ref,
                        *, n_heads, ds):
    for hh in range(n_heads):
        cols = slice(hh * V_DIM, (hh + 1) * V_DIM)
        q2 = jnp.concatenate([qa_ref[:, cols], qb_ref[:, cols]], axis=0)
        s_past = _nt_dot(q2, ck_ref[0, 0, :, cols].astype(BF16))
        s_new = _nt_dot(q2, kn_ref[:, cols])
        m = jnp.maximum(jnp.max(s_past, axis=-1, keepdims=True),
                        jnp.max(s_new, axis=-1, keepdims=True))
        p_past = jnp.exp2(s_past - m)
        p_new = jnp.exp2(s_new - m)
        denom = jnp.sum(p_past, axis=-1, keepdims=True) + jnp.sum(p_new, axis=-1, keepdims=True)
        pv = (_dot(p_past.astype(BF16), cv_ref[0, 0, :, cols].astype(BF16))
              + _dot(p_new.astype(BF16), vn_ref[:, 2 * hh * V_DIM:(2 * hh + 1) * V_DIM]))
        o_ref[:, cols] = _diff_finish(pv / denom, lam_ref[0], sg_ref[...], ds).astype(BF16)


def _attn_sample(lam, qa, qb, kb, vx, cache_k, cache_v, sub_gain, *, layer, n_heads, ds, row0):
    _, db, past, aw = cache_k.shape
    blk0 = row0 // ds
    new = lambda w: pl.BlockSpec((ds, w), lambda d: (blk0 + d, 0))
    cache = pl.BlockSpec((1, 1, past, aw), lambda d: (layer, d, 0, 0))
    vmem = 2 * (2 * past * aw * 4 + 6 * ds * aw * 2) + 8 * 2 * ds * past * 4 + 2 * past * V_DIM * 2
    return pl.pallas_call(
        functools.partial(_attn_sample_kernel, n_heads=n_heads, ds=ds),
        grid=(db,),
        in_specs=[pl.BlockSpec(memory_space=pltpu.SMEM), new(aw), new(aw), new(aw), new(2 * aw),
                  cache, cache, pl.BlockSpec((1, V_DIM), lambda d: (0, 0))],
        out_specs=pl.BlockSpec((ds, aw), lambda d: (d, 0)),
        out_shape=jax.ShapeDtypeStruct((db * ds, aw), BF16),
        compiler_params=_params(1, vmem + 8 * MIB),
        name="attn_sample",
    )(lam, qa, qb, kb, vx, cache_k, cache_v, sub_gain)


def _pool_conv_kernel(u_ref, g_ref, hu_ref, hg_ref, pw_ref, ps_ref, cw_ref, cb_ref, lg_ref, lb_ref, cpw_ref,
                      pool_ref, conv_ref, pu_sc, pg_sc, *, tm, tiles_per_seq, pos0, sub):
    i = pl.program_id(0)
    u = u_ref[...]
    pu_sc[0:POOL_HALO] = hu_ref[0]
    pu_sc[POOL_HALO:POOL_HALO + tm] = u
    pg_sc[0:CONV_HALO] = hg_ref[0]
    pg_sc[CONV_HALO:CONV_HALO + tm] = g_ref[...]

    padded = pu_sc[...]
    sums = []
    acc, width = padded, 1
    for win in POOL_WINDOWS:
        while width < win:
            acc = acc + pltpu.roll(acc, width, axis=0)
            width *= 2
        sums.append(acc[POOL_HALO:POOL_HALO + tm])
    c = u.shape[1]
    group = lax.broadcasted_iota(I32, (tm, c), 1) // (c // len(POOL_WINDOWS))
    pos = pos0 + (i % tiles_per_seq) * tm + lax.broadcasted_iota(I32, (tm, c), 0)
    tot, win_lane = sums[-1], jnp.full((tm, c), POOL_WINDOWS[-1], I32)
    for gi in range(len(POOL_WINDOWS) - 2, -1, -1):
        tot = jnp.where(group == gi, sums[gi], tot)
        win_lane = jnp.where(group == gi, POOL_WINDOWS[gi], win_lane)
    cnt = jnp.minimum(pos + 1, win_lane).astype(F32)
    d = tot / cnt - u
    pool_ref[...] = (_dot(d.astype(BF16), pw_ref[...]) * ps_ref[...]).astype(BF16)

    for r0 in range(0, tm, sub):
        y = jnp.zeros((sub, c), F32)
        for j in range(CONV_K):
            lo = r0 + CONV_HALO - (CONV_K - 1) + j
            y = y + pg_sc[lo:lo + sub, :] * cw_ref[j:j + 1, :]
        y = y + cb_ref[...]
        mu = jnp.mean(y, axis=-1, keepdims=True)
        yc = y - mu
        yn = yc * lax.rsqrt(jnp.mean(yc * yc, axis=-1, keepdims=True) + EPS) * lg_ref[...] + lb_ref[...]
        act = yn * jax.nn.sigmoid(yn)
        conv_ref[r0:r0 + sub, :] = _dot(act.astype(BF16), cpw_ref[...]).astype(BF16)


def _pool_conv(u, g, halo_u, halo_g, pool_wbd, pool_scale, conv_w, conv_b, ln_g, ln_b, conv_pw,
               *, row0, n_rows, tm, tiles_per_seq, pos0):
    c = u.shape[1]
    n_tiles = n_rows // tm
    blk0 = row0 // tm
    sub = _pick_tile(tm, (64, 32))
    row_in = pl.BlockSpec((tm, c), lambda i: (blk0 + i, 0))
    row_out = pl.BlockSpec((tm, c), lambda i: (i, 0))
    full = lambda a: pl.BlockSpec(a.shape, lambda i: (0,) * a.ndim)
    vmem = 2 * (2 * tm * c * 4 + 2 * tm * c * 2) + 14 * (tm + CONV_HALO) * c * 4 + 4 * c * c * 2
    return pl.pallas_call(
        functools.partial(_pool_conv_kernel, tm=tm, tiles_per_seq=tiles_per_seq, pos0=pos0, sub=sub),
        grid=(n_tiles,),
        in_specs=[row_in, row_in,
                  pl.BlockSpec((1, POOL_HALO, c), lambda i: (i, 0, 0)),
                  pl.BlockSpec((1, CONV_HALO, c), lambda i: (i, 0, 0)),
                  full(pool_wbd), full(pool_scale), full(conv_w), full(conv_b), full(ln_g), full(ln_b),
                  full(conv_pw)],
        out_specs=[row_out, row_out],
        out_shape=[jax.ShapeDtypeStruct((n_rows, c), BF16)] * 2,
        scratch_shapes=[pltpu.VMEM((tm + POOL_HALO, c), F32), pltpu.VMEM((tm + CONV_HALO, c), F32)],
        compiler_params=_params(1, vmem + 8 * MIB),
        name="pool_conv",
    )(u, g, halo_u, halo_g, pool_wbd, pool_scale, conv_w, conv_b, ln_g, ln_b, conv_pw)


def _tile_halos(rows, n_seq, seq_len, tm, halo, first):
    c = rows.shape[1]
    tiles = seq_len // tm
    tails = rows.reshape(n_seq, tiles, tm, c)[:, :tiles - 1, tm - halo:, :]
    return jnp.concatenate([first[:, None], tails], axis=1).reshape(n_seq * tiles, halo, c)


def _front_pad(state, halo):
    n, rows, c = state.shape
    return jnp.concatenate([jnp.zeros((n, halo - rows, c), state.dtype), state], axis=1)


def _mix_out(mixer_refs, x_ref, w_ref, xo_ref, n_prompt_tiles):
    def mix(att_ref, pool_ref, conv_ref):
        aw, pw = att_ref.shape[1], pool_ref.shape[1]
        xo_ref[...] = (x_ref[...] + _dot(att_ref[...], w_ref[0:aw, :]) + _dot(pool_ref[...], w_ref[aw:aw + pw, :])
                       + _dot(conv_ref[...], w_ref[aw + pw:, :]))

    is_prompt = pl.program_id(0) < n_prompt_tiles
    pl.when(is_prompt)(lambda: mix(*mixer_refs[:3]))
    pl.when(jnp.logical_not(is_prompt))(lambda: mix(*mixer_refs[3:]))
    return xo_ref[...]


def _out_proj_dense_kernel(*refs, n_prompt_tiles):
    mixer_refs, (x_ref, w_ref, fg_ref, xo_ref, h_ref) = refs[:6], refs[6:]
    x = _mix_out(mixer_refs, x_ref, w_ref, xo_ref, n_prompt_tiles)
    h_ref[...] = _rms_rows(x, fg_ref[...]).astype(BF16)


def _out_proj_router_kernel(*refs, n_prompt_tiles):
    mixer_refs = refs[:6]
    (x_ref, w_ref, fg_ref, rh_ref, rl_ref, tri_ref,
     xo_ref, idx_ref, gate_ref, rank_ref, cnt_ref, carry_sc) = refs[6:]

    @pl.when(pl.program_id(0) == 0)
    def _():
        carry_sc[...] = jnp.zeros(carry_sc.shape, F32)

    x = _mix_out(mixer_refs, x_ref, w_ref, xo_ref, n_prompt_tiles)
    h_hi, h_lo = _split_bf16(_rms_rows(x, fg_ref[...]))
    logits = _nt_dot(rh_ref[...], h_hi) + _nt_dot(rh_ref[...], h_lo) + _nt_dot(rl_ref[...], h_hi)
    n_exp = logits.shape[0]
    eidx = lax.broadcasted_iota(I32, logits.shape, 0)
    m1 = jnp.max(logits, axis=0, keepdims=True)
    i1 = jnp.min(jnp.where(logits == m1, eidx, n_exp), axis=0, keepdims=True)
    rest = jnp.where(eidx == i1, -jnp.inf, logits)
    m2 = jnp.max(rest, axis=0, keepdims=True)
    i2 = jnp.min(jnp.where(rest == m2, eidx, n_exp), axis=0, keepdims=True)
    e2 = jnp.exp(m2 - m1)
    idx_ref[...] = jnp.concatenate([i1, i2], axis=0)
    gate_ref[...] = jnp.concatenate([1.0 / (1.0 + e2), e2 / (1.0 + e2)], axis=0)

    pick1, pick2 = eidx == i1, eidx == i2
    picks = jnp.concatenate([pick1, pick2], axis=0).astype(BF16)
    earlier = _dot(picks, tri_ref[...])
    tot1 = jnp.sum(pick1.astype(F32), axis=1, keepdims=True)
    tot2 = jnp.sum(pick2.astype(F32), axis=1, keepdims=True)
    carry = carry_sc[...]
    base = carry[:, 0:1]
    r1 = jnp.sum(jnp.where(pick1, base + earlier[:n_exp], 0.0), axis=0, keepdims=True)
    r2 = jnp.sum(jnp.where(pick2, base + tot1 + earlier[n_exp:], 0.0), axis=0, keepdims=True)
    rank_ref[...] = jnp.concatenate([r1, r2], axis=0).astype(I32)
    carry = carry + tot1 + tot2
    carry_sc[...] = carry
    cnt_ref[...] = carry.astype(I32)


def _out_proj(mix_prompt, mix_sample, x, w_out, ffn_g, router=None):
    t, d = x.shape
    tp = mix_prompt[0].shape[0]
    tm = _pick_tile(math.gcd(tp, t - tp), (512, 256, 128))
    npt = tp // tm
    row = lambda a: pl.BlockSpec((tm, a.shape[1]), lambda i: (i, 0))
    rows_p = lambda a: pl.BlockSpec((tm, a.shape[1]), lambda i: (jnp.minimum(i, npt - 1), 0))
    rows_s = lambda a: pl.BlockSpec((tm, a.shape[1]), lambda i: (jnp.maximum(i - npt, 0), 0))
    full = lambda a: pl.BlockSpec(a.shape, lambda i: (0,) * a.ndim)
    ins = list(mix_prompt) + list(mix_sample) + [x, w_out, ffn_g]
    in_specs = [rows_p(a) for a in mix_prompt] + [rows_s(a) for a in mix_sample] + [row(x), full(w_out), full(ffn_g)]
    vmem = 2 * (2 * tm * d * 2 + tm * d * 4 * 2 + d * d * 2 + tm * d * 2) + 6 * tm * d * 4
    scratch = []
    if router is None:
        kern = _out_proj_dense_kernel
        out_specs = [row(x), row(x)]
        out_shape = [jax.ShapeDtypeStruct((t, d), F32), jax.ShapeDtypeStruct((t, d), BF16)]
    else:
        kern = _out_proj_router_kernel
        n_exp = router[0].shape[0]
        earlier_token = jnp.arange(tm, dtype=I32)[:, None] < jnp.arange(tm, dtype=I32)[None, :]
        ins += list(router) + [earlier_token.astype(BF16)]
        in_specs += [full(router[0]), full(router[1]), pl.BlockSpec((tm, tm), lambda i: (0, 0))]
        top = pl.BlockSpec((TOP_K, tm), lambda i: (0, i))
        out_specs = [row(x), top, top, top, pl.BlockSpec((n_exp, LANES), lambda i: (0, 0))]
        out_shape = [jax.ShapeDtypeStruct((t, d), F32), jax.ShapeDtypeStruct((TOP_K, t), I32),
                     jax.ShapeDtypeStruct((TOP_K, t), F32), jax.ShapeDtypeStruct((TOP_K, t), I32),
                     jax.ShapeDtypeStruct((n_exp, LANES), I32)]
        scratch = [pltpu.VMEM((n_exp, LANES), F32)]
        vmem += 2 * tm * tm * 2
    return pl.pallas_call(
        functools.partial(kern, n_prompt_tiles=npt),
        grid=(t // tm,), in_specs=in_specs, out_specs=out_specs, out_shape=out_shape,
        scratch_shapes=scratch, compiler_params=_params(1, vmem + 8 * MIB), name="out_proj",
    )(*ins)


def _swiglu_acc(h, acc, wg_ref, wu_ref, wd_ref, lead, tf):
    f = wg_ref.shape[-1]
    for c0 in range(0, f, tf):
        gate = _dot(h, wg_ref[lead + (slice(None), slice(c0, c0 + tf))])
        up = _dot(h, wu_ref[lead + (slice(None), slice(c0, c0 + tf))])
        act = (gate * jax.nn.sigmoid(gate) * up).astype(BF16)
        acc = acc + _dot(act, wd_ref[lead + (slice(c0, c0 + tf), slice(None))])
    return acc


def _ffn_kernel(h_ref, x_ref, wg_ref, wu_ref, wd_ref, o_ref, *, tf):
    o_ref[...] = _swiglu_acc(h_ref[...], x_ref[...], wg_ref, wu_ref, wd_ref, (), tf)


def _ffn(h, x, wg, wu, wd):
    t, d = x.shape
    f = wg.shape[1]
    tm = _pick_tile(t, (512, 256, 128))
    tf = _pick_tile(f, (512, 256, 128))
    row = lambda: pl.BlockSpec((tm, d), lambda i: (i, 0))
    full = lambda a: pl.BlockSpec(a.shape, lambda i: (0,) * a.ndim)
    vmem = 2 * (3 * d * f * 2 + tm * d * (2 + 4 + 4)) + 4 * tm * tf * 4 + 2 * tm * d * 4
    return pl.pallas_call(
        functools.partial(_ffn_kernel, tf=tf),
        grid=(t // tm,),
        in_specs=[row(), row(), full(wg), full(wu), full(wd)],
        out_specs=row(),
        out_shape=jax.ShapeDtypeStruct((t, d), F32),
        compiler_params=_params(1, vmem + 4 * MIB),
        name="ffn",
    )(h, x, wg, wu, wd)


def _moe_kernel(te_ref, tv_ref, src_ref, src_next_ref, dst_ref, fg_ref, wg_ref, wu_ref, wd_ref, x_hbm,
                out_hbm, xbuf, ybuf, gsem, ssem, *, tm, tf, n_tiles):
    i = pl.program_id(0)
    slot = i % 2

    def gather_row(idx_ref, r, s):
        return pltpu.make_async_copy(x_hbm.at[pl.ds(idx_ref[0, 0, r], 1)], xbuf.at[s, pl.ds(r, 1)], gsem.at[s])

    def scatter_row(row, r, s):
        return pltpu.make_async_copy(ybuf.at[s, pl.ds(r, 1)], out_hbm.at[pl.ds(row, 1)], ssem.at[s])

    def for_rows(fn, static=False):
        if static:
            for r in range(tm):
                fn(r)
            return

        def body(r, carry):
            fn(r)
            return carry
        lax.fori_loop(0, tm, body, 0, unroll=8)

    @pl.when(i == 0)
    def _():
        for_rows(lambda r: gather_row(src_ref, r, slot).start())

    def for_static_slot(fn, when=True):
        for s in (0, 1):
            pl.when(jnp.logical_and(when, slot == s))(functools.partial(fn, s))

    for_static_slot(lambda s: for_rows(lambda r: gather_row(src_next_ref, r, 1 - s).start(), static=True),
                    when=i + 1 < n_tiles)

    for_rows(lambda r: gather_row(src_ref, r, slot).wait())

    @pl.when(i >= 2)
    def _():
        for_rows(lambda r: scatter_row(0, r, slot).wait())

    @pl.when(tv_ref[i] > 0)
    def _():
        h = _rms_rows(xbuf[slot], fg_ref[...]).astype(BF16)
        lead = (0,)
        ybuf[slot] = _swiglu_acc(h, jnp.zeros((tm, h.shape[1]), F32), wg_ref, wu_ref, wd_ref, lead, tf)

    @pl.when(tv_ref[i] == 0)
    def _():
        ybuf[slot] = jnp.zeros(ybuf.shape[1:], F32)

    for_static_slot(lambda s: for_rows(lambda r: scatter_row(dst_ref[0, 0, r], r, s).start(), static=True))

    @pl.when(i == n_tiles - 1)
    def _():
        for_rows(lambda r: scatter_row(0, r, slot).wait())
        if n_tiles > 1:
            for_rows(lambda r: scatter_row(0, r, 1 - slot).wait())


def _moe_route(idx, rank, counts, tm):
    k, t = idx.shape
    a = k * t
    n_exp = counts.shape[0]
    padded = ((counts + tm - 1) // tm) * tm
    ends = jnp.cumsum(padded)
    offs = ends - padded
    pos = rank
    for e in range(n_exp):
        pos = pos + jnp.where(idx == e, offs[e], 0)
    n_pad = a + n_exp * tm
    n_tiles = n_pad // tm
    slot_ids = jnp.arange(n_pad, dtype=I32)
    choice = jnp.full((n_pad,), -1, I32).at[pos.reshape(a)].set(jnp.arange(a, dtype=I32))
    filled = choice >= 0
    src = jnp.where(filled, choice % t, 0)
    spare = a + ((slot_ids // tm) % 2) * tm + slot_ids % tm
    dst = jnp.where(filled, choice, spare)
    starts = jnp.arange(n_tiles, dtype=I32) * tm
    tile_expert = jnp.minimum(jnp.sum((starts[:, None] >= ends[None, :]).astype(I32), axis=1), n_exp - 1)
    tile_valid = (starts < ends[-1]).astype(I32)
    shape3 = (n_tiles, 1, tm)
    return tile_expert, tile_valid, src.reshape(shape3), dst.reshape(shape3)


def _moe(x, idx, gate, rank, counts, ffn_g, wg, wu, wd, split_rows=None):
    t, d = x.shape
    n_exp, _, f = wg.shape
    tm = MOE_TILE_ROWS
    tf = _pick_tile(f, (512, 256, 128))
    te, tv, src, dst = _moe_route(idx, rank, counts, tm)
    n_tiles = src.shape[0]
    n_out = TOP_K * t + 2 * tm
    smem_rows = lambda nxt: pl.BlockSpec(
        (1, 1, tm), (lambda i, te, tv: (jnp.minimum(i + 1, n_tiles - 1), 0, 0)) if nxt else (lambda i, te, tv: (i, 0, 0)),
        memory_space=pltpu.SMEM)
    wspec = lambda a: pl.BlockSpec((1,) + a.shape[1:], lambda i, te, tv: (te[i], 0, 0))
    vmem = 2 * 3 * d * f * 2 + 4 * tm * d * 4 + 4 * tm * tf * 4 + 3 * tm * d * 4
    out2 = pl.pallas_call(
        functools.partial(_moe_kernel, tm=tm, tf=tf, n_tiles=n_tiles),
        grid_spec=pltpu.PrefetchScalarGridSpec(
            num_scalar_prefetch=2, grid=(n_tiles,),
            in_specs=[smem_rows(False), smem_rows(True), smem_rows(False),
                      pl.BlockSpec(ffn_g.shape, lambda i, te, tv: (0, 0)),
                      wspec(wg), wspec(wu), wspec(wd),
                      pl.BlockSpec(memory_space=pl.ANY)],
            out_specs=pl.BlockSpec(memory_space=pl.ANY),
            scratch_shapes=[pltpu.VMEM((2, tm, d), F32), pltpu.VMEM((2, tm, d), F32),
                            pltpu.SemaphoreType.DMA((2,)), pltpu.SemaphoreType.DMA((2,))]),
        out_shape=jax.ShapeDtypeStruct((n_out, d), F32),
        compiler_params=_params(1, vmem + 4 * MIB),
        name="moe_experts",
    )(te, tv, src, src, dst, ffn_g, wg, wu, wd, x)

    t_first = t if split_rows is None else split_rows
    tc = _pick_tile(math.gcd(t_first, t - t_first) if split_rows else t, (512, 256, 128))
    blocks = t // tc
    n_first = t_first // tc
    if split_rows is None:
        out_specs = [pl.BlockSpec((tc, d), lambda i: (i, 0))]
        out_shape = [jax.ShapeDtypeStruct((t, d), F32)]
    else:
        out_specs = [pl.BlockSpec((tc, d), lambda i: (jnp.minimum(i, n_first - 1), 0)),
                     pl.BlockSpec((tc, d), lambda i: (jnp.maximum(i - n_first, 0), 0))]
        out_shape = [jax.ShapeDtypeStruct((t_first, d), F32), jax.ShapeDtypeStruct((t - t_first, d), F32)]
    outs = pl.pallas_call(
        functools.partial(_moe_combine_kernel, n_first_blocks=n_first),
        grid=(blocks,),
        in_specs=[pl.BlockSpec((tc, d), lambda i: (i, 0)), pl.BlockSpec((tc, TOP_K), lambda i: (i, 0))]
        + [pl.BlockSpec((tc, d), functools.partial(lambda i, kk: (kk * blocks + i, 0), kk=kk)) for kk in range(TOP_K)],
        out_specs=out_specs,
        out_shape=out_shape,
        compiler_params=_params(1, 2 * 5 * tc * d * 4 + 4 * MIB),
        name="moe_combine",
    )(x, gate.T, *([out2] * TOP_K))
    return outs[0] if split_rows is None else tuple(outs)


def _moe_combine_kernel(x_ref, g_ref, *refs, n_first_blocks):
    y_refs, o_refs = refs[:TOP_K], refs[TOP_K:]
    acc = x_ref[...]
    g = g_ref[...]
    for kk, y_ref in enumerate(y_refs):
        acc = acc + g[:, kk:kk + 1] * y_ref[...]
    if len(o_refs) == 1:
        o_refs[0][...] = acc
        return
    is_first = pl.program_id(0) < n_first_blocks

    @pl.when(is_first)
    def _():
        o_refs[0][...] = acc

    @pl.when(jnp.logical_not(is_first))
    def _():
        o_refs[1][...] = acc


def kernel(x_prompt, x_sample, cache_k, cache_v, state_pool, state_conv, attn_norm_g, w_in, q_norm_g, k_norm_g,
           lambda_q1, lambda_k1, lambda_q2, lambda_k2, subln_g, pool_w, pool_scale, conv_w, conv_b, conv_ln_g,
           conv_ln_b, conv_pw, w_out, ffn_norm_g, ffn_w_gate, ffn_w_up, ffn_w_down, router_w, moe_w_gate,
           moe_w_up, moe_w_down):
    bp, sp, d = x_prompt.shape
    db, ds, _ = x_sample.shape
    depth = w_in.shape[0]
    past = cache_k.shape[2]
    n_sub = cache_k.shape[3]
    n_heads = n_sub // 2
    aw = n_heads * V_DIM
    pw = state_pool.shape[-1]
    cw = state_conv.shape[-1]
    pool_rows, conv_rows = state_pool.shape[2], state_conv.shape[2]
    tp, ts = bp * sp, db * ds
    assert pool_rows == max(POOL_WINDOWS) - 1 and conv_rows == CONV_K - 1
    assert w_in.shape[2] == 3 * aw + pw + 2 * cw and ds >= conv_rows and tp % ds == 0

    x = jnp.concatenate([x_prompt.reshape(tp, d), x_sample.reshape(ts, d)], axis=0)
    ck = cache_k.reshape(depth, db, past, aw)
    cv = cache_v.reshape(depth, db, past, aw)

    lane_group = jnp.arange(aw, dtype=I32) // HEAD_DIM
    gsum = (lane_group[:, None] == lane_group[None, :]).astype(BF16)
    n_groups = len(POOL_WINDOWS)
    tm_pool = _pick_tile(sp, (512, 256, 128))

    outs = {name: [] for name in ("kp", "vp", "pp", "cp", "ks", "vs", "ps", "cs")}
    for l in range(depth):
        lam_init = 0.8 - 0.6 * math.exp(-0.3 * l)
        lam = (jnp.exp(jnp.sum(lambda_q1[l] * lambda_k1[l])) - jnp.exp(jnp.sum(lambda_q2[l] * lambda_k2[l]))
               + lam_init).astype(F32).reshape(1)
        qg = (jnp.tile(q_norm_g[l], n_sub) * (HEAD_DIM ** -0.5 * math.log2(math.e))).reshape(1, aw)
        kg = jnp.tile(k_norm_g[l], n_sub).reshape(1, aw)
        sub_gain = (subln_g[l] * (1.0 - lam_init)).reshape(1, V_DIM)

        (qa, qb, kb, vx, u, g), (kp, vp, ks, vs) = _in_proj(
            x, attn_norm_g[l].reshape(1, d), w_in[l].astype(BF16), gsum, qg, kg, tp=tp, aw=aw, pw=pw, cw=cw)

        score_bound = HEAD_DIM * jnp.max(jnp.abs(qg)) * jnp.max(jnp.abs(kg)) * 1.02
        att_p = _attn_prompt(lam, qa, qb, kb, vx, sub_gain, score_bound, n_heads=n_heads, b=bp, s=sp)
        att_s = _attn_sample(lam, qa, qb, kb, vx, ck, cv, sub_gain, layer=l, n_heads=n_heads, ds=ds, row0=tp)

        pool_wbd = jax.scipy.linalg.block_diag(*[pool_w[l, gi] for gi in range(n_groups)]).astype(BF16)
        mix_w = (pool_wbd, pool_scale[l].reshape(1, pw), conv_w[l], conv_b[l].reshape(1, cw),
                 conv_ln_g[l].reshape(1, cw), conv_ln_b[l].reshape(1, cw), conv_pw[l].astype(BF16))
        zeros_p = lambda halo, c: jnp.zeros((bp, halo, c), F32)
        pool_p, conv_p = _pool_conv(
            u, g, _tile_halos(u[:tp], bp, sp, tm_pool, POOL_HALO, zeros_p(POOL_HALO, pw)),
            _tile_halos(g[:tp], bp, sp, tm_pool, CONV_HALO, zeros_p(CONV_HALO, cw)), *mix_w,
            row0=0, n_rows=tp, tm=tm_pool, tiles_per_seq=sp // tm_pool, pos0=0)
        pool_s, conv_s = _pool_conv(
            u, g, _front_pad(state_pool[l], POOL_HALO), _front_pad(state_conv[l], CONV_HALO), *mix_w,
            row0=tp, n_rows=ts, tm=ds, tiles_per_seq=1, pos0=past)
        mixed = ((att_p, pool_p, conv_p), (att_s, pool_s, conv_s))

        w_o = w_out[l].astype(BF16)
        fg = ffn_norm_g[l].reshape(1, d)
        i = l // 2
        if l % 2 == 0:
            x, h = _out_proj(*mixed, x, w_o, fg)
            x = _ffn(h, x, ffn_w_gate[i].astype(BF16), ffn_w_up[i].astype(BF16), ffn_w_down[i].astype(BF16))
        else:
            x, idx, gate, rank, counts = _out_proj(*mixed, x, w_o, fg, router=_split_bf16(router_w[i].T))
            x = _moe(x, idx, gate, rank, counts[:, 0], fg, moe_w_gate[i].astype(BF16), moe_w_up[i].astype(BF16),
                     moe_w_down[i].astype(BF16), split_rows=tp if l == depth - 1 else None)

        outs["kp"].append(kp.reshape(bp, sp, n_sub, HEAD_DIM))
        outs["vp"].append(vp.reshape(bp, sp, n_heads, V_DIM))
        outs["pp"].append(u[:tp].reshape(bp, sp, pw)[:, sp - pool_rows:])
        outs["cp"].append(g[:tp].reshape(bp, sp, cw)[:, sp - conv_rows:])
        outs["ks"].append(ks.reshape(db, ds, n_sub, HEAD_DIM))
        outs["vs"].append(vs.reshape(db, ds, n_heads, V_DIM))
        outs["ps"].append(u[tp:].reshape(db, ds, pw)[:, ds - pool_rows:])
        outs["cs"].append(g[tp:].reshape(db, ds, cw)[:, ds - conv_rows:])

    stack = lambda name: jnp.stack(outs[name])
    y_prompt, y_sample = x if isinstance(x, tuple) else (x[:tp], x[tp:])
    return (y_prompt.reshape(bp, sp, d), y_sample.reshape(db, ds, d),
            stack("kp"), stack("vp"), stack("pp"), stack("cp"),
            stack("ks"), stack("vs"), stack("ps"), stack("cs"))
```
